```python
import jax, jax.numpy as jnp
from jax import lax
import numpy as np

D_MODEL = 1024
BATCH = 8
SEQ = 2048
DEPTH = 2
DEC_BATCH = 4
DEC_SEQ = 4096
PAST_LEN = 128

N_MIXERS = 2
D_FF = 2816
FFN_RESIDUAL_WEIGHT = 0.5
CONV_WIDTH = 31
CONV_HALF = CONV_WIDTH // 2
N_HEADS = 8
QK_NOPE_DIM = 128
QK_ROPE_DIM = 64
V_HEAD_DIM = 128
Q_LORA_RANK = 384
KV_LORA_RANK = 256
ROPE_THETA = 10000.0
Q_BLOCK = 128
NORM_EPS = 1e-6

kernel_name = 'hybrid_conformer_mla_encoder'


def rms_norm(x, g):
    xf = x.astype(jnp.float32)
    y = xf * lax.rsqrt(jnp.mean(xf * xf, axis=-1, keepdims=True) + NORM_EPS)
    return (y * g.astype(jnp.float32)).astype(x.dtype)


def layer_norm(x, g, b):
    xf = x.astype(jnp.float32)
    mu = jnp.mean(xf, axis=-1, keepdims=True)
    var = jnp.mean(jnp.square(xf - mu), axis=-1, keepdims=True)
    y = (xf - mu) * lax.rsqrt(var + NORM_EPS)
    return (y * g.astype(jnp.float32) + b.astype(jnp.float32)).astype(x.dtype)


def swiglu(x, w_gate, w_up, w_down):
    return (jax.nn.silu(x @ w_gate) * (x @ w_up)) @ w_down


def conv_module(x, w_pw1, b_pw1, w_dw, b_dw, ln_g, ln_b, w_pw2, b_pw2):
    h = x @ w_pw1 + b_pw1
    h = h[..., :D_MODEL] * jax.nn.sigmoid(h[..., D_MODEL:])
    h = lax.conv_general_dilated(
        h, w_dw[:, None, :], window_strides=(1,), padding=[(CONV_HALF, CONV_HALF)],
        dimension_numbers=('NWC', 'WIO', 'NWC'), feature_group_count=D_MODEL) + b_dw
    h = jax.nn.silu(layer_norm(h, ln_g, ln_b))
    return h @ w_pw2 + b_pw2


def rope_tables(seq):
    pos = jnp.arange(seq, dtype=jnp.float32)
    inv_freq = ROPE_THETA ** (-jnp.arange(0, QK_ROPE_DIM, 2, dtype=jnp.float32) / QK_ROPE_DIM)
    ang = pos[:, None] * inv_freq[None, :]
    return jnp.cos(ang), jnp.sin(ang)


def apply_rope(x, cos, sin):
    half = QK_ROPE_DIM // 2
    x1, x2 = x[..., :half], x[..., half:]
    out = jnp.concatenate([x1 * cos - x2 * sin, x2 * cos + x1 * sin], axis=-1)
    return out.astype(x.dtype)


def mla(x, w_dq, q_norm_g, w_uq, w_dkv, kv_norm_g, w_uk, w_uv, w_o):
    B, S, _ = x.shape
    cos, sin = rope_tables(S)
    c_q = rms_norm(x @ w_dq, q_norm_g)
    q = (c_q @ w_uq).reshape(B, S, N_HEADS, QK_NOPE_DIM + QK_ROPE_DIM)
    q_nope = q[..., :QK_NOPE_DIM]
    q_rope = apply_rope(q[..., QK_NOPE_DIM:], cos[:, None, :], sin[:, None, :])
    kv_a = x @ w_dkv
    c_kv = rms_norm(kv_a[..., :KV_LORA_RANK], kv_norm_g)
    k_rope = apply_rope(kv_a[..., KV_LORA_RANK:], cos, sin)
    k_nope = (c_kv @ w_uk).reshape(B, S, N_HEADS, QK_NOPE_DIM)
    v = (c_kv @ w_uv).reshape(B, S, N_HEADS, V_HEAD_DIM)
    scale = (QK_NOPE_DIM + QK_ROPE_DIM) ** -0.5
    n_blk = S // Q_BLOCK
    qn_blocks = q_nope.reshape(B, n_blk, Q_BLOCK, N_HEADS, QK_NOPE_DIM).transpose(1, 0, 2, 3, 4)
    qr_blocks = q_rope.reshape(B, n_blk, Q_BLOCK, N_HEADS, QK_ROPE_DIM).transpose(1, 0, 2, 3, 4)

    def attend(blk):
        qn, qr = blk
        s = (jnp.einsum('bqhd,bkhd->bhqk', qn, k_nope, preferred_element_type=jnp.float32)
             + jnp.einsum('bqhr,bkr->bhqk', qr, k_rope, preferred_element_type=jnp.float32))
        p = jax.nn.softmax(s * scale, axis=-1).astype(v.dtype)
        return jnp.einsum('bhqk,bkhd->bqhd', p, v)

    o = lax.map(attend, (qn_blocks, qr_blocks))
    o = o.transpose(1, 0, 2, 3, 4).reshape(B, S, N_HEADS * V_HEAD_DIM)
    return o @ w_o


def encoder_layer(x, ffn1, mix_pre_g, mixer_fn, mixer_params, mix_post_g, ffn2):
    pre1, wg1, wu1, wd1, post1 = ffn1
    x = x + FFN_RESIDUAL_WEIGHT * rms_norm(swiglu(rms_norm(x, pre1), wg1, wu1, wd1), post1)
    x = x + rms_norm(mixer_fn(rms_norm(x, mix_pre_g), *mixer_params), mix_post_g)
    pre2, wg2, wu2, wd2, post2 = ffn2
    x = x + FFN_RESIDUAL_WEIGHT * rms_norm(swiglu(rms_norm(x, pre2), wg2, wu2, wd2), post2)
    return x


def setup_inputs(seed: int = 0) -> dict:
    key = jax.random.key(seed)
    ks = iter(jax.random.split(key, 64))

    def w(shape, fan_in):
        return jax.random.normal(next(ks), shape, jnp.float32) * fan_in ** -0.5

    def g(n):
        return 1.0 + 0.01 * jax.random.normal(next(ks), (n,), jnp.float32)

    def b(n):
        return 0.01 * jax.random.normal(next(ks), (n,), jnp.float32)

    p = {}
    p['x_prompt'] = jax.random.normal(next(ks), (BATCH, SEQ, D_MODEL), jnp.float32)
    p['x_sample'] = jax.random.normal(next(ks), (DEC_BATCH, DEC_SEQ, D_MODEL), jnp.float32)

    def add_ffn(prefix):
        p[prefix + '_pre_g'] = g(D_MODEL)
        p[prefix + '_w_gate'] = w((D_MODEL, D_FF), D_MODEL)
        p[prefix + '_w_up'] = w((D_MODEL, D_FF), D_MODEL)
        p[prefix + '_w_down'] = w((D_FF, D_MODEL), D_FF)
        p[prefix + '_post_g'] = g(D_MODEL)

    add_ffn('l0_ffn1')
    p['l0_mix_pre_g'] = g(D_MODEL)
    p['l0_conv_w_pw1'] = w((D_MODEL, 2 * D_MODEL), D_MODEL)
    p['l0_conv_b_pw1'] = b(2 * D_MODEL)
    p['l0_conv_w_dw'] = w((CONV_WIDTH, D_MODEL), CONV_WIDTH)
    p['l0_conv_b_dw'] = b(D_MODEL)
    p['l0_conv_ln_g'] = g(D_MODEL)
    p['l0_conv_ln_b'] = b(D_MODEL)
    p['l0_conv_w_pw2'] = w((D_MODEL, D_MODEL), D_MODEL)
    p['l0_conv_b_pw2'] = b(D_MODEL)
    p['l0_mix_post_g'] = g(D_MODEL)
    add_ffn('l0_ffn2')
    add_ffn('l1_ffn1')
    p['l1_mix_pre_g'] = g(D_MODEL)
    p['l1_mla_w_dq'] = w((D_MODEL, Q_LORA_RANK), D_MODEL)
    p['l1_mla_q_norm_g'] = g(Q_LORA_RANK)
    p['l1_mla_w_uq'] = w((Q_LORA_RANK, N_HEADS * (QK_NOPE_DIM + QK_ROPE_DIM)), Q_LORA_RANK)
    p['l1_mla_w_dkv'] = w((D_MODEL, KV_LORA_RANK + QK_ROPE_DIM), D_MODEL)
    p['l1_mla_kv_norm_g'] = g(KV_LORA_RANK)
    p['l1_mla_w_uk'] = w((KV_LORA_RANK, N_HEADS * QK_NOPE_DIM), KV_LORA_RANK)
    p['l1_mla_w_uv'] = w((KV_LORA_RANK, N_HEADS * V_HEAD_DIM), KV_LORA_RANK)
    p['l1_mla_w_o'] = w((N_HEADS * V_HEAD_DIM, D_MODEL), N_HEADS * V_HEAD_DIM)
    p['l1_mix_post_g'] = g(D_MODEL)
    add_ffn('l1_ffn2')
    return p


def reference(x_prompt, x_sample,
              l0_ffn1_pre_g, l0_ffn1_w_gate, l0_ffn1_w_up, l0_ffn1_w_down, l0_ffn1_post_g,
              l0_mix_pre_g, l0_conv_w_pw1, l0_conv_b_pw1, l0_conv_w_dw, l0_conv_b_dw,
              l0_conv_ln_g, l0_conv_ln_b, l0_conv_w_pw2, l0_conv_b_pw2, l0_mix_post_g,
              l0_ffn2_pre_g, l0_ffn2_w_gate, l0_ffn2_w_up, l0_ffn2_w_down, l0_ffn2_post_g,
              l1_ffn1_pre_g, l1_ffn1_w_gate, l1_ffn1_w_up, l1_ffn1_w_down, l1_ffn1_post_g,
              l1_mix_pre_g, l1_mla_w_dq, l1_mla_q_norm_g, l1_mla_w_uq, l1_mla_w_dkv,
              l1_mla_kv_norm_g, l1_mla_w_uk, l1_mla_w_uv, l1_mla_w_o, l1_mix_post_g,
              l1_ffn2_pre_g, l1_ffn2_w_gate, l1_ffn2_w_up, l1_ffn2_w_down, l1_ffn2_post_g):
    layers = (
        ((l0_ffn1_pre_g, l0_ffn1_w_gate, l0_ffn1_w_up, l0_ffn1_w_down, l0_ffn1_post_g),
         l0_mix_pre_g,
         (l0_conv_w_pw1, l0_conv_b_pw1, l0_conv_w_dw, l0_conv_b_dw,
          l0_conv_ln_g, l0_conv_ln_b, l0_conv_w_pw2, l0_conv_b_pw2),
         l0_mix_post_g,
         (l0_ffn2_pre_g, l0_ffn2_w_gate, l0_ffn2_w_up, l0_ffn2_w_down, l0_ffn2_post_g)),
        ((l1_ffn1_pre_g, l1_ffn1_w_gate, l1_ffn1_w_up, l1_ffn1_w_down, l1_ffn1_post_g),
         l1_mix_pre_g,
         (l1_mla_w_dq, l1_mla_q_norm_g, l1_mla_w_uq, l1_mla_w_dkv,
          l1_mla_kv_norm_g, l1_mla_w_uk, l1_mla_w_uv, l1_mla_w_o),
         l1_mix_post_g,
         (l1_ffn2_pre_g, l1_ffn2_w_gate, l1_ffn2_w_up, l1_ffn2_w_down, l1_ffn2_post_g)),
    )
    mixers = (conv_module, mla)

    def trunk(x):
        for i in range(DEPTH):
            ffn1, mix_pre_g, mixer_params, mix_post_g, ffn2 = layers[i]
            x = encoder_layer(x, ffn1, mix_pre_g, mixers[i % N_MIXERS], mixer_params, mix_post_g, ffn2)
        return x

    y_prompt = trunk(x_prompt)
    y_sample = trunk(x_sample)
    return (y_prompt, y_sample)
```

```python
import functools

import jax
import jax.numpy as jnp
from jax import lax
from jax.experimental import pallas as pl
from jax.experimental.pallas import tpu as pltpu

F32 = jnp.float32
BF16 = jnp.bfloat16

D_MODEL = 1024
D_FF = 2816
FFN_RESIDUAL_WEIGHT = 0.5
CONV_WIDTH = 31
CONV_HALF = CONV_WIDTH // 2
N_HEADS = 8
QK_NOPE_DIM = 128
QK_ROPE_DIM = 64
ROPE_HALF = QK_ROPE_DIM // 2
V_HEAD_DIM = 128
Q_LORA_RANK = 384
KV_LORA_RANK = 256
ROPE_THETA = 10000.0
NORM_EPS = 1e-6

LANES = 128
MXU_DIM = 256
QK_PAD_DIM = 256
VMEM_LIMIT_BYTES = 56 * 1024 * 1024

FFN_CHUNK = MXU_DIM
FFN_NUM_CHUNKS = D_FF // FFN_CHUNK
FFN_TOKENS = 512
CONV_TOKENS = 512
CONV_HALO = 16
CONV_ROW_BLOCK = 32
CONV_LANE_BLOCK = 256
MLA_TOKENS = 512
ATTN_Q_BLOCK = 256


def _rms(x, g):
    ms = jnp.mean(x * x, axis=-1, keepdims=True)
    return x * lax.rsqrt(ms + NORM_EPS) * g


def _silu(x):
    return x / (1.0 + jnp.exp(-x))


def _sigmoid(x):
    return 1.0 / (1.0 + jnp.exp(-x))


def _dot(a, b):
    return jnp.dot(a, b, preferred_element_type=F32)


def _const_spec(shape):
    zeros = (0,) * len(shape)
    return pl.BlockSpec(shape, lambda *_: zeros, pipeline_mode=pl.Buffered(1))


def _ffn_body(x_ref, pre_ref, wg_ref, wu_ref, wd_ref, post_ref, o_ref, xn_ref, acc_ref):
    xn_ref[...] = _rms(x_ref[...], pre_ref[...]).astype(BF16)
    acc_ref[...] = jnp.zeros_like(acc_ref)

    def chunk(c, carry):
        xn = xn_ref[...]
        g = _dot(xn, wg_ref[c])
        u = _dot(xn, wu_ref[c])
        h = (_silu(g) * u).astype(BF16)
        acc_ref[...] += _dot(h, wd_ref[c])
        return carry

    lax.fori_loop(0, FFN_NUM_CHUNKS, chunk, 0)
    o_ref[...] = x_ref[...] + FFN_RESIDUAL_WEIGHT * _rms(acc_ref[...], post_ref[...])


def _ffn(x2d, p):
    tokens = x2d.shape[0]
    tm = FFN_TOKENS
    row = pl.BlockSpec((tm, D_MODEL), lambda i: (i, 0))
    return pl.pallas_call(
        _ffn_body,
        grid=(tokens // tm,),
        in_specs=[
            row,
            _const_spec((1, D_MODEL)),
            _const_spec((FFN_NUM_CHUNKS, D_MODEL, FFN_CHUNK)),
            _const_spec((FFN_NUM_CHUNKS, D_MODEL, FFN_CHUNK)),
            _const_spec((FFN_NUM_CHUNKS, FFN_CHUNK, D_MODEL)),
            _const_spec((1, D_MODEL)),
        ],
        out_specs=row,
        out_shape=jax.ShapeDtypeStruct((tokens, D_MODEL), F32),
        scratch_shapes=[pltpu.VMEM((tm, D_MODEL), BF16), pltpu.VMEM((tm, D_MODEL), F32)],
        compiler_params=pltpu.CompilerParams(
            dimension_semantics=("parallel",), vmem_limit_bytes=VMEM_LIMIT_BYTES),
        name="ffn",
    )(x2d, p["pre"], p["wg"], p["wu"], p["wd"], p["post"])


def _prep_ffn(pre_g, w_gate, w_up, w_down, post_g):
    def cols(w):
        return w.astype(BF16).reshape(D_MODEL, FFN_NUM_CHUNKS, FFN_CHUNK).transpose(1, 0, 2)
    return dict(
        pre=pre_g.reshape(1, D_MODEL), post=post_g.reshape(1, D_MODEL),
        wg=cols(w_gate), wu=cols(w_up),
        wd=w_down.astype(BF16).reshape(FFN_NUM_CHUNKS, FFN_CHUNK, D_MODEL))


def _conv_body(seq_len, xm_ref, xp_ref, xq_ref, pre_ref, w1_ref, b1_ref, wdw_ref, bdw_ref,
               lng_ref, lnb_ref, w2_ref, b2_ref, post_ref, o_ref, xn_ref, h_ref, c_ref):
    tm = CONV_TOKENS
    halo = CONV_HALO
    rows = tm + 2 * halo
    i = pl.program_id(1)
    pre = pre_ref[...]
    xn_ref[0:halo, :] = _rms(xp_ref[0], pre).astype(BF16)
    xn_ref[halo:halo + tm, :] = _rms(xm_ref[0], pre).astype(BF16)
    xn_ref[halo + tm:rows, :] = _rms(xq_ref[0], pre).astype(BF16)

    tok = i * tm - halo + lax.broadcasted_iota(jnp.int32, (rows, 1), 0)
    inside = jnp.logical_and(tok >= 0, tok < seq_len)
    lb = CONV_LANE_BLOCK
    for n in range(D_MODEL // lb):
        xn = xn_ref[...]
        val = _dot(xn, w1_ref[:, n * lb:(n + 1) * lb]) + b1_ref[:, n * lb:(n + 1) * lb]
        gate = (_dot(xn, w1_ref[:, D_MODEL + n * lb:D_MODEL + (n + 1) * lb])
                + b1_ref[:, D_MODEL + n * lb:D_MODEL + (n + 1) * lb])
        h = jnp.where(inside, val * _sigmoid(gate), 0.0)
        for j in range(lb // LANES):
            h_ref[n * (lb // LANES) + j] = h[:, j * LANES:(j + 1) * LANES]

    rb = CONV_ROW_BLOCK
    shift = halo - CONV_HALF
    for j in range(D_MODEL // LANES):
        lanes = slice(j * LANES, (j + 1) * LANES)

        def row_block(r, carry, j=j, lanes=lanes):
            r0 = pl.multiple_of(r * rb, rb)
            acc = jnp.broadcast_to(bdw_ref[:, lanes], (rb, LANES))
            for k in range(CONV_WIDTH):
                acc = acc + h_ref[j, pl.ds(r0 + shift + k, rb), :] * wdw_ref[k:k + 1, lanes]
            c_ref[pl.ds(r0, rb), lanes] = acc
            return carry

        lax.fori_loop(0, tm // rb, row_block, 0)

    c = c_ref[...]
    mu = jnp.mean(c, axis=-1, keepdims=True)
    cc = c - mu
    var = jnp.mean(cc * cc, axis=-1, keepdims=True)
    y = cc * lax.rsqrt(var + NORM_EPS) * lng_ref[...] + lnb_ref[...]
    z = _dot(_silu(y).astype(BF16), w2_ref[...]) + b2_ref[...]
    o_ref[0] = xm_ref[0] + _rms(z, post_ref[...])


def _conv(x, p):
    batch, seq_len, _ = x.shape
    tm = CONV_TOKENS
    halo = CONV_HALO
    per_tile = tm // halo
    last_halo_block = seq_len // halo - 1
    main = pl.BlockSpec((1, tm, D_MODEL), lambda b, i: (b, i, 0))
    prev = pl.BlockSpec((1, halo, D_MODEL),
                        lambda b, i: (b, jnp.maximum(i * per_tile - 1, 0), 0))
    nxt = pl.BlockSpec((1, halo, D_MODEL),
                       lambda b, i: (b, jnp.minimum((i + 1) * per_tile, last_halo_block), 0))
    return pl.pallas_call(
        functools.partial(_conv_body, seq_len),
        grid=(batch, seq_len // tm),
        in_specs=[
            main, prev, nxt,
            _const_spec((1, D_MODEL)),
            _const_spec((D_MODEL, 2 * D_MODEL)),
            _const_spec((1, 2 * D_MODEL)),
            _const_spec((CONV_WIDTH, D_MODEL)),
            _const_spec((1, D_MODEL)),
            _const_spec((1, D_MODEL)),
            _const_spec((1, D_MODEL)),
            _const_spec((D_MODEL, D_MODEL)),
            _const_spec((1, D_MODEL)),
            _const_spec((1, D_MODEL)),
        ],
        out_specs=main,
        out_shape=jax.ShapeDtypeStruct(x.shape, F32),
        scratch_shapes=[
            pltpu.VMEM((tm + 2 * halo, D_MODEL), BF16),
            pltpu.VMEM((D_MODEL // LANES, tm + 2 * halo, LANES), F32),
            pltpu.VMEM((tm, D_MODEL), F32),
        ],
        compiler_params=pltpu.CompilerParams(
            dimension_semantics=("parallel", "parallel"), vmem_limit_bytes=VMEM_LIMIT_BYTES),
        name="conv_module",
    )(x, x, x, p["pre"], p["w1"], p["b1"], p["wdw"], p["bdw"], p["lng"], p["lnb"],
      p["w2"], p["b2"], p["post"])


def _prep_conv(pre_g, w_pw1, b_pw1, w_dw, b_dw, ln_g, ln_b, w_pw2, b_pw2, post_g):
    row = lambda v: v.reshape(1, -1)
    return dict(pre=row(pre_g), w1=w_pw1.astype(BF16), b1=row(b_pw1), wdw=w_dw, bdw=row(b_dw),
                lng=row(ln_g), lnb=row(ln_b), w2=w_pw2.astype(BF16), b2=row(b_pw2),
                post=row(post_g))


def _mla_proj_body(x_ref, pre_ref, wdq_ref, qg_ref, wqa_ref, wqb_ref, wkv_ref, kvg_ref,
                   wuk_ref, wuv_ref, cos_ref, sin_ref, q_ref, k_ref, v_ref):
    scale = (QK_NOPE_DIM + QK_ROPE_DIM) ** -0.5
    xn = _rms(x_ref[0], pre_ref[...]).astype(BF16)
    cq = _rms(_dot(xn, wdq_ref[...]), qg_ref[...]).astype(BF16)
    kv = _dot(xn, wkv_ref[...])
    ckv = _rms(kv[:, :KV_LORA_RANK], kvg_ref[...]).astype(BF16)
    cos = cos_ref[...]
    sin = sin_ref[...]
    k_rope = (kv[:, KV_LORA_RANK:KV_LORA_RANK + LANES] * cos
              + kv[:, KV_LORA_RANK + LANES:] * sin).astype(BF16)
    v_ref[0] = _dot(ckv, wuv_ref[...]).astype(BF16)
    for h in range(N_HEADS):
        lo = h * QK_PAD_DIM
        qa = _dot(cq, wqa_ref[:, lo:lo + QK_PAD_DIM])
        qb = _dot(cq, wqb_ref[:, h * LANES:(h + 1) * LANES])
        q_ref[0, :, lo:lo + LANES] = (qa[:, :LANES] * scale).astype(BF16)
        q_ref[0, :, lo + LANES:lo + QK_PAD_DIM] = (
            (qa[:, LANES:] * cos + qb * sin) * scale).astype(BF16)
        k_ref[0, :, lo:lo + LANES] = _dot(
            ckv, wuk_ref[:, h * QK_NOPE_DIM:(h + 1) * QK_NOPE_DIM]).astype(BF16)
        k_ref[0, :, lo + LANES:lo + QK_PAD_DIM] = k_rope


def _attn_body(q_ref, k_ref, v_ref, o_ref):
    s = lax.dot_general(q_ref[0], k_ref[0], (((1,), (1,)), ((), ())),
                        preferred_element_type=F32)
    m = jnp.max(s, axis=-1, keepdims=True)
    p = jnp.exp(s - m)
    l = jnp.sum(p, axis=-1, keepdims=True)
    o = _dot(p.astype(BF16), v_ref[0])
    o_ref[0] = (o / l).astype(BF16)


def _mla_out_body(a_ref, x_ref, wo_ref, post_ref, o_ref):
    o_ref[...] = x_ref[...] + _rms(_dot(a_ref[...], wo_ref[...]), post_ref[...])


def _mla(x, p, cos_tab, sin_tab):
    batch, seq_len, _ = x.shape
    tm = MLA_TOKENS
    qk_width = N_HEADS * QK_PAD_DIM
    v_width = N_HEADS * V_HEAD_DIM
    tile = lambda w: pl.BlockSpec((1, tm, w), lambda b, i: (b, i, 0))
    tab = pl.BlockSpec((tm, LANES), lambda b, i: (i, 0))
    q, k, v = pl.pallas_call(
        _mla_proj_body,
        grid=(batch, seq_len // tm),
        in_specs=[
            tile(D_MODEL),
            _const_spec((1, D_MODEL)),
            _const_spec((D_MODEL, Q_LORA_RANK)),
            _const_spec((1, Q_LORA_RANK)),
            _const_spec((Q_LORA_RANK, qk_width)),
            _const_spec((Q_LORA_RANK, N_HEADS * LANES)),
            _const_spec((D_MODEL, KV_LORA_RANK + 2 * LANES)),
            _const_spec((1, KV_LORA_RANK)),
            _const_spec((KV_LORA_RANK, N_HEADS * QK_NOPE_DIM)),
            _const_spec((KV_LORA_RANK, v_width)),
            tab, tab,
        ],
        out_specs=[tile(qk_width), tile(qk_width), tile(v_width)],
        out_shape=[
            jax.ShapeDtypeStruct((batch, seq_len, qk_width), BF16),
            jax.ShapeDtypeStruct((batch, seq_len, qk_width), BF16),
            jax.ShapeDtypeStruct((batch, seq_len, v_width), BF16),
        ],
        compiler_params=pltpu.CompilerParams(
            dimension_semantics=("parallel", "parallel"), vmem_limit_bytes=VMEM_LIMIT_BYTES),
        name="mla_proj",
    )(x, p["pre"], p["wdq"], p["qg"], p["wqa"], p["wqb"], p["wkv"], p["kvg"], p["wuk"],
      p["wuv"], cos_tab, sin_tab)

    tq = ATTN_Q_BLOCK
    attn = pl.pallas_call(
        _attn_body,
        grid=(batch, N_HEADS, seq_len // tq),
        in_specs=[
            pl.BlockSpec((1, tq, QK_PAD_DIM), lambda b, h, i: (b, i, h)),
            pl.BlockSpec((1, seq_len, QK_PAD_DIM), lambda b, h, i: (b, 0, h)),
            pl.BlockSpec((1, seq_len, V_HEAD_DIM), lambda b, h, i: (b, 0, h)),
        ],
        out_specs=pl.BlockSpec((1, tq, V_HEAD_DIM), lambda b, h, i: (b, i, h)),
        out_shape=jax.ShapeDtypeStruct((batch, seq_len, v_width), BF16),
        compiler_params=pltpu.CompilerParams(
            dimension_semantics=("parallel", "parallel", "parallel"),
            vmem_limit_bytes=VMEM_LIMIT_BYTES),
        name="mla_attention",
    )(q, k, v)

    tokens = batch * seq_len
    row = lambda w: pl.BlockSpec((tm, w), lambda i: (i, 0))
    out = pl.pallas_call(
        _mla_out_body,
        grid=(tokens // tm,),
        in_specs=[row(v_width), row(D_MODEL), _const_spec((v_width, D_MODEL)),
                  _const_spec((1, D_MODEL))],
        out_specs=row(D_MODEL),
        out_shape=jax.ShapeDtypeStruct((tokens, D_MODEL), F32),
        compiler_params=pltpu.CompilerParams(
            dimension_semantics=("parallel",), vmem_limit_bytes=VMEM_LIMIT_BYTES),
        name="mla_out",
    )(attn.reshape(tokens, v_width), x.reshape(tokens, D_MODEL), p["wo"], p["post"])
    return out.reshape(x.shape)


def _prep_mla(pre_g, w_dq, q_norm_g, w_uq, w_dkv, kv_norm_g, w_uk, w_uv, w_o, post_g):
    row = lambda v: v.reshape(1, -1)
    head_dim = QK_NOPE_DIM + QK_ROPE_DIM
    wq = w_uq.reshape(Q_LORA_RANK, N_HEADS, head_dim)
    q_x1 = wq[:, :, QK_NOPE_DIM:QK_NOPE_DIM + ROPE_HALF]
    q_x2 = wq[:, :, QK_NOPE_DIM + ROPE_HALF:]
    pad = lambda n: jnp.zeros((Q_LORA_RANK, N_HEADS, n), F32)
    wqa = jnp.concatenate([wq, pad(QK_PAD_DIM - head_dim)], axis=-1)
    wqb = jnp.concatenate([-q_x2, q_x1, pad(LANES - QK_ROPE_DIM)], axis=-1)
    k_x1 = w_dkv[:, KV_LORA_RANK:KV_LORA_RANK + ROPE_HALF]
    k_x2 = w_dkv[:, KV_LORA_RANK + ROPE_HALF:]
    kpad = jnp.zeros((D_MODEL, LANES - QK_ROPE_DIM), F32)
    wkv = jnp.concatenate([w_dkv[:, :KV_LORA_RANK], k_x1, k_x2, kpad, -k_x2, k_x1, kpad], axis=-1)
    return dict(
        pre=row(pre_g), wdq=w_dq.astype(BF16), qg=row(q_norm_g),
        wqa=wqa.reshape(Q_LORA_RANK, N_HEADS * QK_PAD_DIM).astype(BF16),
        wqb=wqb.reshape(Q_LORA_RANK, N_HEADS * LANES).astype(BF16),
        wkv=wkv.astype(BF16), kvg=row(kv_norm_g), wuk=w_uk.astype(BF16), wuv=w_uv.astype(BF16),
        wo=w_o.astype(BF16), post=row(post_g))


def _rope_tables(seq_len):
    pos = jnp.arange(seq_len, dtype=F32)
    inv_freq = ROPE_THETA ** (-jnp.arange(0, QK_ROPE_DIM, 2, dtype=F32) / QK_ROPE_DIM)
    ang = pos[:, None] * inv_freq[None, :]
    zeros = jnp.zeros((seq_len, LANES - QK_ROPE_DIM), F32)
    cos, sin = jnp.cos(ang), jnp.sin(ang)
    return (jnp.concatenate([cos, cos, zeros], axis=-1),
            jnp.concatenate([sin, sin, zeros], axis=-1))


def kernel(x_prompt, x_sample, l0_ffn1_pre_g, l0_ffn1_w_gate, l0_ffn1_w_up, l0_ffn1_w_down, l0_ffn1_post_g, l0_mix_pre_g, l0_conv_w_pw1, l0_conv_b_pw1, l0_conv_w_dw, l0_conv_b_dw, l0_conv_ln_g, l0_conv_ln_b, l0_conv_w_pw2, l0_conv_b_pw2, l0_mix_post_g, l0_ffn2_pre_g, l0_ffn2_w_gate, l0_ffn2_w_up, l0_ffn2_w_down, l0_ffn2_post_g, l1_ffn1_pre_g, l1_ffn1_w_gate, l1_ffn1_w_up, l1_ffn1_w_down, l1_ffn1_post_g, l1_mix_pre_g, l1_mla_w_dq, l1_mla_q_norm_g, l1_mla_w_uq, l1_mla_w_dkv, l1_mla_kv_norm_g, l1_mla_w_uk, l1_mla_w_uv, l1_mla_w_o, l1_mix_post_g, l1_ffn2_pre_g, l1_ffn2_w_gate, l1_ffn2_w_up, l1_ffn2_w_down, l1_ffn2_post_g):
    l0_ffn1 = _prep_ffn(l0_ffn1_pre_g, l0_ffn1_w_gate, l0_ffn1_w_up, l0_ffn1_w_down, l0_ffn1_post_g)
    l0_ffn2 = _prep_ffn(l0_ffn2_pre_g, l0_ffn2_w_gate, l0_ffn2_w_up, l0_ffn2_w_down, l0_ffn2_post_g)
    l1_ffn1 = _prep_ffn(l1_ffn1_pre_g, l1_ffn1_w_gate, l1_ffn1_w_up, l1_ffn1_w_down, l1_ffn1_post_g)
    l1_ffn2 = _prep_ffn(l1_ffn2_pre_g, l1_ffn2_w_gate, l1_ffn2_w_up, l1_ffn2_w_down, l1_ffn2_post_g)
    conv = _prep_conv(l0_mix_pre_g, l0_conv_w_pw1, l0_conv_b_pw1, l0_conv_w_dw, l0_conv_b_dw,
                      l0_conv_ln_g, l0_conv_ln_b, l0_conv_w_pw2, l0_conv_b_pw2, l0_mix_post_g)
    mla = _prep_mla(l1_mix_pre_g, l1_mla_w_dq, l1_mla_q_norm_g, l1_mla_w_uq, l1_mla_w_dkv,
                    l1_mla_kv_norm_g, l1_mla_w_uk, l1_mla_w_uv, l1_mla_w_o, l1_mix_post_g)

    def trunk(x):
        shape = x.shape
        flat = lambda a: a.reshape(-1, D_MODEL)
        cos_tab, sin_tab = _rope_tables(shape[1])
        x = _ffn(flat(x), l0_ffn1).reshape(shape)
        x = _conv(x, conv)
        x = _ffn(flat(x), l0_ffn2)
        x = _ffn(x, l1_ffn1).reshape(shape)
        x = _mla(x, mla, cos_tab, sin_tab)
        x = _ffn(flat(x), l1_ffn2).reshape(shape)
        return x

    return (trunk(x_prompt), trunk(x_sample))
```

```python
import functools

import jax
import jax.numpy as jnp
from jax import lax
from jax.experimental import pallas as pl
from jax.experimental.pallas import tpu as pltpu

F32 = jnp.float32
BF16 = jnp.bfloat16

D_MODEL = 1024
D_FF = 2816
FFN_RESIDUAL_WEIGHT = 0.5
CONV_WIDTH = 31
CONV_HALF = CONV_WIDTH // 2
N_HEADS = 8
QK_NOPE_DIM = 128
QK_ROPE_DIM = 64
ROPE_HALF = QK_ROPE_DIM // 2
V_HEAD_DIM = 128
Q_LORA_RANK = 384
KV_LORA_RANK = 256
ROPE_THETA = 10000.0
NORM_EPS = 1e-6

LANES = 128
MXU_DIM = 256
QK_PAD_DIM = 256
VMEM_LIMIT_BYTES = 56 * 1024 * 1024

FFN_CHUNK = MXU_DIM
FFN_NUM_CHUNKS = D_FF // FFN_CHUNK
FFN_TOKENS = 512
CONV_TOKENS = 512
CONV_HALO = 16
CONV_ROW_BLOCK = 64
CONV_LANE_BLOCK = 256
MLA_TOKENS = 512
ATTN_Q_BLOCK = 512
ATTN_KV_BLOCK = 1024
ATTN_Q_BLOCKS_PER_STEP = 4
LOG2_E = 1.4426950408889634


def _rms(x, g):
    ms = jnp.mean(x * x, axis=-1, keepdims=True)
    return x * lax.rsqrt(ms + NORM_EPS) * g


def _silu(x):
    return x / (1.0 + jnp.exp(-x))


def _sigmoid(x):
    return 1.0 / (1.0 + jnp.exp(-x))


def _dot(a, b):
    return jnp.dot(a, b, preferred_element_type=F32)


def _const_spec(shape):
    zeros = (0,) * len(shape)
    return pl.BlockSpec(shape, lambda *_: zeros, pipeline_mode=pl.Buffered(1))


def _ffn_body(x_ref, pre_ref, wg_ref, wu_ref, wd_ref, post_ref, o_ref, xn_ref):
    xn_ref[...] = _rms(x_ref[...], pre_ref[...]).astype(BF16)
    acc = None
    for c in range(FFN_NUM_CHUNKS):
        cols = slice(c * FFN_CHUNK, (c + 1) * FFN_CHUNK)
        xn = xn_ref[...]
        g = _dot(xn, wg_ref[:, cols])
        u = _dot(xn, wu_ref[:, cols])
        h = (_silu(g) * u).astype(BF16)
        part = _dot(h, wd_ref[cols, :])
        acc = part if acc is None else acc + part
    o_ref[...] = x_ref[...] + FFN_RESIDUAL_WEIGHT * _rms(acc, post_ref[...])


def _ffn(x2d, p):
    tokens = x2d.shape[0]
    tm = FFN_TOKENS
    row = pl.BlockSpec((tm, D_MODEL), lambda i: (i, 0))
    return pl.pallas_call(
        _ffn_body,
        grid=(tokens // tm,),
        in_specs=[
            row,
            _const_spec((1, D_MODEL)),
            _const_spec((D_MODEL, D_FF)),
            _const_spec((D_MODEL, D_FF)),
            _const_spec((D_FF, D_MODEL)),
            _const_spec((1, D_MODEL)),
        ],
        out_specs=row,
        out_shape=jax.ShapeDtypeStruct((tokens, D_MODEL), F32),
        scratch_shapes=[pltpu.VMEM((tm, D_MODEL), BF16)],
        compiler_params=pltpu.CompilerParams(
            dimension_semantics=("parallel",), vmem_limit_bytes=VMEM_LIMIT_BYTES),
        name="ffn",
    )(x2d, p["pre"], p["wg"], p["wu"], p["wd"], p["post"])


def _prep_ffn(pre_g, w_gate, w_up, w_down, post_g):
    return dict(
        pre=pre_g.reshape(1, D_MODEL), post=post_g.reshape(1, D_MODEL),
        wg=w_gate.astype(BF16), wu=w_up.astype(BF16), wd=w_down.astype(BF16))


def _conv_body(seq_len, xm_ref, xp_ref, xq_ref, pre_ref, w1_ref, b1_ref, wdw_ref, bdw_ref,
               lng_ref, lnb_ref, w2_ref, b2_ref, post_ref, o_ref, xn_ref, h_ref, c_ref):
    tm = CONV_TOKENS
    halo = CONV_HALO
    rows = tm + 2 * halo
    i = pl.program_id(1)
    pre = pre_ref[...]
    xn_ref[0:halo, :] = _rms(xp_ref[0], pre).astype(BF16)
    xn_ref[halo:halo + tm, :] = _rms(xm_ref[0], pre).astype(BF16)
    xn_ref[halo + tm:rows, :] = _rms(xq_ref[0], pre).astype(BF16)

    tok = i * tm - halo + lax.broadcasted_iota(jnp.int32, (rows, 1), 0)
    inside = jnp.logical_and(tok >= 0, tok < seq_len)
    lb = CONV_LANE_BLOCK
    for n in range(D_MODEL // lb):
        xn = xn_ref[...]
        val = _dot(xn, w1_ref[:, n * lb:(n + 1) * lb]) + b1_ref[:, n * lb:(n + 1) * lb]
        gate = (_dot(xn, w1_ref[:, D_MODEL + n * lb:D_MODEL + (n + 1) * lb])
                + b1_ref[:, D_MODEL + n * lb:D_MODEL + (n + 1) * lb])
        h = jnp.where(inside, val * _sigmoid(gate), 0.0)
        for j in range(lb // LANES):
            h_ref[n * (lb // LANES) + j] = h[:, j * LANES:(j + 1) * LANES]

    rb = CONV_ROW_BLOCK
    shift = halo - CONV_HALF
    for j in range(D_MODEL // LANES):
        lanes = slice(j * LANES, (j + 1) * LANES)

        def row_block(r, carry, j=j, lanes=lanes):
            r0 = pl.multiple_of(r * rb, rb)
            acc = jnp.broadcast_to(bdw_ref[:, lanes], (rb, LANES))
            for k in range(CONV_WIDTH):
                acc = acc + h_ref[j, pl.ds(r0 + shift + k, rb), :] * wdw_ref[k:k + 1, lanes]
            c_ref[pl.ds(r0, rb), lanes] = acc
            return carry

        lax.fori_loop(0, tm // rb, row_block, 0)

    c = c_ref[...]
    mu = jnp.mean(c, axis=-1, keepdims=True)
    cc = c - mu
    var = jnp.mean(cc * cc, axis=-1, keepdims=True)
    y = cc * lax.rsqrt(var + NORM_EPS) * lng_ref[...] + lnb_ref[...]
    z = _dot(_silu(y).astype(BF16), w2_ref[...]) + b2_ref[...]
    o_ref[0] = xm_ref[0] + _rms(z, post_ref[...])


def _conv(x, p):
    batch, seq_len, _ = x.shape
    tm = CONV_TOKENS
    halo = CONV_HALO
    per_tile = tm // halo
    last_halo_block = seq_len // halo - 1
    main = pl.BlockSpec((1, tm, D_MODEL), lambda b, i: (b, i, 0))
    prev = pl.BlockSpec((1, halo, D_MODEL),
                        lambda b, i: (b, jnp.maximum(i * per_tile - 1, 0), 0))
    nxt = pl.BlockSpec((1, halo, D_MODEL),
                       lambda b, i: (b, jnp.minimum((i + 1) * per_tile, last_halo_block), 0))
    return pl.pallas_call(
        functools.partial(_conv_body, seq_len),
        grid=(batch, seq_len // tm),
        in_specs=[
            main, prev, nxt,
            _const_spec((1, D_MODEL)),
            _const_spec((D_MODEL, 2 * D_MODEL)),
            _const_spec((1, 2 * D_MODEL)),
            _const_spec((CONV_WIDTH, D_MODEL)),
            _const_spec((1, D_MODEL)),
            _const_spec((1, D_MODEL)),
            _const_spec((1, D_MODEL)),
            _const_spec((D_MODEL, D_MODEL)),
            _const_spec((1, D_MODEL)),
            _const_spec((1, D_MODEL)),
        ],
        out_specs=main,
        out_shape=jax.ShapeDtypeStruct(x.shape, F32),
        scratch_shapes=[
            pltpu.VMEM((tm + 2 * halo, D_MODEL), BF16),
            pltpu.VMEM((D_MODEL // LANES, tm + 2 * halo, LANES), F32),
            pltpu.VMEM((tm, D_MODEL), F32),
        ],
        compiler_params=pltpu.CompilerParams(
            dimension_semantics=("parallel", "parallel"), vmem_limit_bytes=VMEM_LIMIT_BYTES),
        name="conv_module",
    )(x, x, x, p["pre"], p["w1"], p["b1"], p["wdw"], p["bdw"], p["lng"], p["lnb"],
      p["w2"], p["b2"], p["post"])


def _prep_conv(pre_g, w_pw1, b_pw1, w_dw, b_dw, ln_g, ln_b, w_pw2, b_pw2, post_g):
    row = lambda v: v.reshape(1, -1)
    return dict(pre=row(pre_g), w1=w_pw1.astype(BF16), b1=row(b_pw1), wdw=w_dw, bdw=row(b_dw),
                lng=row(ln_g), lnb=row(ln_b), w2=w_pw2.astype(BF16), b2=row(b_pw2),
                post=row(post_g))


def _mla_proj_body(x_ref, pre_ref, wdq_ref, qg_ref, wqa_ref, wqb_ref, wkv_ref, kvg_ref,
                   wuk_ref, wuv_ref, cos_ref, sin_ref, cos_t_ref, sin_t_ref,
                   qt_ref, k_ref, vt_ref):
    scale = (QK_NOPE_DIM + QK_ROPE_DIM) ** -0.5 * LOG2_E
    nt = (((1,), (1,)), ((), ()))
    xn = _rms(x_ref[0], pre_ref[...]).astype(BF16)
    cq = _rms(_dot(xn, wdq_ref[...]), qg_ref[...]).astype(BF16)
    kv = _dot(xn, wkv_ref[...])
    ckv = _rms(kv[:, :KV_LORA_RANK], kvg_ref[...]).astype(BF16)
    k_rope = (kv[:, KV_LORA_RANK:KV_LORA_RANK + LANES] * cos_ref[...]
              + kv[:, KV_LORA_RANK + LANES:] * sin_ref[...]).astype(BF16)
    k_nope = _dot(ckv, wuk_ref[...])
    for h in range(N_HEADS):
        lo = h * QK_PAD_DIM
        k_ref[0, :, lo:lo + LANES] = k_nope[:, h * QK_NOPE_DIM:(h + 1) * QK_NOPE_DIM].astype(BF16)
        k_ref[0, :, lo + LANES:lo + QK_PAD_DIM] = k_rope
    vt_ref[0] = lax.dot_general(wuv_ref[...], ckv, nt, preferred_element_type=F32).astype(BF16)
    cos_t = cos_t_ref[...]
    sin_t = sin_t_ref[...]
    qb_all = lax.dot_general(wqb_ref[...], cq, nt, preferred_element_type=F32)
    heads_per_dot = N_HEADS // 2
    for half in range(2):
        rows = heads_per_dot * QK_PAD_DIM
        qa_half = lax.dot_general(wqa_ref[half * rows:(half + 1) * rows, :], cq, nt,
                                  preferred_element_type=F32)
        for hh in range(heads_per_dot):
            h = half * heads_per_dot + hh
            lo = h * QK_PAD_DIM
            qa = qa_half[hh * QK_PAD_DIM:(hh + 1) * QK_PAD_DIM]
            qb = qb_all[h * LANES:(h + 1) * LANES]
            qt_ref[0, lo:lo + LANES, :] = (qa[:LANES] * scale).astype(BF16)
            qt_ref[0, lo + LANES:lo + QK_PAD_DIM, :] = (
                (qa[LANES:] * cos_t + qb * sin_t) * scale).astype(BF16)


def _attn_body(seq_len, qt_ref, k_ref, vt_ref, o_ref):
    tq = ATTN_Q_BLOCK
    n_kv = seq_len // ATTN_KV_BLOCK
    stages = [(qb, j) for qb in range(ATTN_Q_BLOCKS_PER_STEP) for j in range(n_kv)]

    def scores(stage):
        qb, j = stage
        return _dot(k_ref[0, j * ATTN_KV_BLOCK:(j + 1) * ATTN_KV_BLOCK, :],
                    qt_ref[0, :, qb * tq:(qb + 1) * tq])

    m = l = acc = None
    s_next = scores(stages[0])
    for idx, (qb, j) in enumerate(stages):
        s = s_next
        if idx + 1 < len(stages):
            s_next = scores(stages[idx + 1])
        vt = vt_ref[0, :, j * ATTN_KV_BLOCK:(j + 1) * ATTN_KV_BLOCK]
        s_max = jnp.max(s, axis=0, keepdims=True)
        if j == 0:
            m = s_max
            p = jnp.exp2(s - m)
            l = jnp.sum(p, axis=0, keepdims=True)
            acc = _dot(vt, p.astype(BF16))
        else:
            m_new = jnp.maximum(m, s_max)
            alpha = jnp.exp2(m - m_new)
            p = jnp.exp2(s - m_new)
            l = alpha * l + jnp.sum(p, axis=0, keepdims=True)
            acc = alpha * acc + _dot(vt, p.astype(BF16))
            m = m_new
        if j == n_kv - 1:
            o_ref[0, qb * tq:(qb + 1) * tq, :] = (acc / l).T.astype(BF16)


def _mla_out_body(a_ref, x_ref, wo_ref, post_ref, o_ref):
    o_ref[...] = x_ref[...] + _rms(_dot(a_ref[...], wo_ref[...]), post_ref[...])


def _mla(x, p, tabs):
    batch, seq_len, _ = x.shape
    tm = MLA_TOKENS
    qk_width = N_HEADS * QK_PAD_DIM
    v_width = N_HEADS * V_HEAD_DIM
    tile = lambda w: pl.BlockSpec((1, tm, w), lambda b, i: (b, i, 0))
    tile_t = lambda w: pl.BlockSpec((1, w, tm), lambda b, i: (b, 0, i))
    tab = pl.BlockSpec((tm, LANES), lambda b, i: (i, 0))
    tab_t = pl.BlockSpec((LANES, tm), lambda b, i: (0, i))
    qt, k, vt = pl.pallas_call(
        _mla_proj_body,
        grid=(batch, seq_len // tm),
        in_specs=[
            tile(D_MODEL),
            _const_spec((1, D_MODEL)),
            _const_spec((D_MODEL, Q_LORA_RANK)),
            _const_spec((1, Q_LORA_RANK)),
            _const_spec((qk_width, Q_LORA_RANK)),
            _const_spec((N_HEADS * LANES, Q_LORA_RANK)),
            _const_spec((D_MODEL, KV_LORA_RANK + 2 * LANES)),
            _const_spec((1, KV_LORA_RANK)),
            _const_spec((KV_LORA_RANK, N_HEADS * QK_NOPE_DIM)),
            _const_spec((v_width, KV_LORA_RANK)),
            tab, tab, tab_t, tab_t,
        ],
        out_specs=[tile_t(qk_width), tile(qk_width), tile_t(v_width)],
        out_shape=[
            jax.ShapeDtypeStruct((batch, qk_width, seq_len), BF16),
            jax.ShapeDtypeStruct((batch, seq_len, qk_width), BF16),
            jax.ShapeDtypeStruct((batch, v_width, seq_len), BF16),
        ],
        compiler_params=pltpu.CompilerParams(
            dimension_semantics=("parallel", "parallel"), vmem_limit_bytes=VMEM_LIMIT_BYTES),
        name="mla_proj",
    )(x, p["pre"], p["wdq"], p["qg"], p["wqa"], p["wqb"], p["wkv"], p["kvg"], p["wuk"],
      p["wuv"], *tabs)

    tq = ATTN_Q_BLOCK * ATTN_Q_BLOCKS_PER_STEP
    attn = pl.pallas_call(
        functools.partial(_attn_body, seq_len),
        grid=(batch, N_HEADS, seq_len // tq),
        in_specs=[
            pl.BlockSpec((1, QK_PAD_DIM, tq), lambda b, h, i: (b, h, i)),
            pl.BlockSpec((1, seq_len, QK_PAD_DIM), lambda b, h, i: (b, 0, h)),
            pl.BlockSpec((1, V_HEAD_DIM, seq_len), lambda b, h, i: (b, h, 0)),
        ],
        out_specs=pl.BlockSpec((1, tq, V_HEAD_DIM), lambda b, h, i: (b, i, h)),
        out_shape=jax.ShapeDtypeStruct((batch, seq_len, v_width), BF16),
        compiler_params=pltpu.CompilerParams(
            dimension_semantics=("parallel", "parallel", "parallel"),
            vmem_limit_bytes=VMEM_LIMIT_BYTES),
        name="mla_attention",
    )(qt, k, vt)

    tokens = batch * seq_len
    row = lambda w: pl.BlockSpec((tm, w), lambda i: (i, 0))
    out = pl.pallas_call(
        _mla_out_body,
        grid=(tokens // tm,),
        in_specs=[row(v_width), row(D_MODEL), _const_spec((v_width, D_MODEL)),
                  _const_spec((1, D_MODEL))],
        out_specs=row(D_MODEL),
        out_shape=jax.ShapeDtypeStruct((tokens, D_MODEL), F32),
        compiler_params=pltpu.CompilerParams(
            dimension_semantics=("parallel",), vmem_limit_bytes=VMEM_LIMIT_BYTES),
        name="mla_out",
    )(attn.reshape(tokens, v_width), x.reshape(tokens, D_MODEL), p["wo"], p["post"])
    return out.reshape(x.shape)


def _prep_mla(pre_g, w_dq, q_norm_g, w_uq, w_dkv, kv_norm_g, w_uk, w_uv, w_o, post_g):
    row = lambda v: v.reshape(1, -1)
    head_dim = QK_NOPE_DIM + QK_ROPE_DIM
    wq = w_uq.reshape(Q_LORA_RANK, N_HEADS, head_dim)
    q_x1 = wq[:, :, QK_NOPE_DIM:QK_NOPE_DIM + ROPE_HALF]
    q_x2 = wq[:, :, QK_NOPE_DIM + ROPE_HALF:]
    pad = lambda n: jnp.zeros((Q_LORA_RANK, N_HEADS, n), F32)
    wqa = jnp.concatenate([wq, pad(QK_PAD_DIM - head_dim)], axis=-1)
    wqb = jnp.concatenate([-q_x2, q_x1, pad(LANES - QK_ROPE_DIM)], axis=-1)
    k_x1 = w_dkv[:, KV_LORA_RANK:KV_LORA_RANK + ROPE_HALF]
    k_x2 = w_dkv[:, KV_LORA_RANK + ROPE_HALF:]
    kpad = jnp.zeros((D_MODEL, LANES - QK_ROPE_DIM), F32)
    wkv = jnp.concatenate([w_dkv[:, :KV_LORA_RANK], k_x1, k_x2, kpad, -k_x2, k_x1, kpad], axis=-1)
    return dict(
        pre=row(pre_g), wdq=w_dq.astype(BF16), qg=row(q_norm_g),
        wqa=wqa.reshape(Q_LORA_RANK, N_HEADS * QK_PAD_DIM).T.astype(BF16),
        wqb=wqb.reshape(Q_LORA_RANK, N_HEADS * LANES).T.astype(BF16),
        wkv=wkv.astype(BF16), kvg=row(kv_norm_g), wuk=w_uk.astype(BF16),
        wuv=w_uv.T.astype(BF16), wo=w_o.astype(BF16), post=row(post_g))


def _rope_tables(seq_len):
    pos = jnp.arange(seq_len, dtype=F32)
    inv_freq = ROPE_THETA ** (-jnp.arange(0, QK_ROPE_DIM, 2, dtype=F32) / QK_ROPE_DIM)
    ang = pos[:, None] * inv_freq[None, :]
    zeros = jnp.zeros((seq_len, LANES - QK_ROPE_DIM), F32)
    cos, sin = jnp.cos(ang), jnp.sin(ang)
    cos_tab = jnp.concatenate([cos, cos, zeros], axis=-1)
    sin_tab = jnp.concatenate([sin, sin, zeros], axis=-1)
    return cos_tab, sin_tab, cos_tab.T, sin_tab.T


def kernel(x_prompt, x_sample, l0_ffn1_pre_g, l0_ffn1_w_gate, l0_ffn1_w_up, l0_ffn1_w_down, l0_ffn1_post_g, l0_mix_pre_g, l0_conv_w_pw1, l0_conv_b_pw1, l0_conv_w_dw, l0_conv_b_dw, l0_conv_ln_g, l0_conv_ln_b, l0_conv_w_pw2, l0_conv_b_pw2, l0_mix_post_g, l0_ffn2_pre_g, l0_ffn2_w_gate, l0_ffn2_w_up, l0_ffn2_w_down, l0_ffn2_post_g, l1_ffn1_pre_g, l1_ffn1_w_gate, l1_ffn1_w_up, l1_ffn1_w_down, l1_ffn1_post_g, l1_mix_pre_g, l1_mla_w_dq, l1_mla_q_norm_g, l1_mla_w_uq, l1_mla_w_dkv, l1_mla_kv_norm_g, l1_mla_w_uk, l1_mla_w_uv, l1_mla_w_o, l1_mix_post_g, l1_ffn2_pre_g, l1_ffn2_w_gate, l1_ffn2_w_up, l1_ffn2_w_down, l1_ffn2_post_g):
    l0_ffn1 = _prep_ffn(l0_ffn1_pre_g, l0_ffn1_w_gate, l0_ffn1_w_up, l0_ffn1_w_down, l0_ffn1_post_g)
    l0_ffn2 = _prep_ffn(l0_ffn2_pre_g, l0_ffn2_w_gate, l0_ffn2_w_up, l0_ffn2_w_down, l0_ffn2_post_g)
    l1_ffn1 = _prep_ffn(l1_ffn1_pre_g, l1_ffn1_w_gate, l1_ffn1_w_up, l1_ffn1_w_down, l1_ffn1_post_g)
    l1_ffn2 = _prep_ffn(l1_ffn2_pre_g, l1_ffn2_w_gate, l1_ffn2_w_up, l1_ffn2_w_down, l1_ffn2_post_g)
    conv = _prep_conv(l0_mix_pre_g, l0_conv_w_pw1, l0_conv_b_pw1, l0_conv_w_dw, l0_conv_b_dw,
                      l0_conv_ln_g, l0_conv_ln_b, l0_conv_w_pw2, l0_conv_b_pw2, l0_mix_post_g)
    mla = _prep_mla(l1_mix_pre_g, l1_mla_w_dq, l1_mla_q_norm_g, l1_mla_w_uq, l1_mla_w_dkv,
                    l1_mla_kv_norm_g, l1_mla_w_uk, l1_mla_w_uv, l1_mla_w_o, l1_mix_post_g)

    def trunk(x):
        shape = x.shape
        flat = lambda a: a.reshape(-1, D_MODEL)
        tabs = _rope_tables(shape[1])
        x = _ffn(flat(x), l0_ffn1).reshape(shape)
        x = _conv(x, conv)
        x = _ffn(flat(x), l0_ffn2)
        x = _ffn(x, l1_ffn1).reshape(shape)
        x = _mla(x, mla, tabs)
        x = _ffn(flat(x), l1_ffn2).reshape(shape)
        return x

    return (trunk(x_prompt), trunk(x_sample))
```

```python
import functools

import jax
import jax.numpy as jnp
from jax import lax
from jax.experimental import pallas as pl
from jax.experimental.pallas import tpu as pltpu

F32 = jnp.float32
BF16 = jnp.bfloat16

D_MODEL = 1024
D_FF = 2816
FFN_RESIDUAL_WEIGHT = 0.5
CONV_WIDTH = 31
CONV_HALF = CONV_WIDTH // 2
N_HEADS = 8
QK_NOPE_DIM = 128
QK_ROPE_DIM = 64
ROPE_HALF = QK_ROPE_DIM // 2
V_HEAD_DIM = 128
Q_LORA_RANK = 384
KV_LORA_RANK = 256
ROPE_THETA = 10000.0
NORM_EPS = 1e-6

LANES = 128
MXU_DIM = 256
QK_PAD_DIM = 256
VMEM_LIMIT_BYTES = 56 * 1024 * 1024

FFN_CHUNK = MXU_DIM
FFN_NUM_CHUNKS = D_FF // FFN_CHUNK
FFN_TOKENS = 512
CONV_TOKENS = 512
CONV_HALO = 16
CONV_ROW_BLOCK = 64
CONV_LANE_BLOCK = 256
MLA_TOKENS = 512
ATTN_Q_BLOCK = 512
ATTN_KV_BLOCK = 1024
ATTN_Q_BLOCKS_PER_STEP = 4
LOG2_E = 1.4426950408889634


def _rms(x, g):
    ms = jnp.mean(x * x, axis=-1, keepdims=True)
    return x * lax.rsqrt(ms + NORM_EPS) * g


def _silu(x):
    return x / (1.0 + jnp.exp(-x))


def _sigmoid(x):
    return 1.0 / (1.0 + jnp.exp(-x))


def _dot(a, b):
    return jnp.dot(a, b, preferred_element_type=F32)


def _const_spec(shape):
    zeros = (0,) * len(shape)
    return pl.BlockSpec(shape, lambda *_: zeros, pipeline_mode=pl.Buffered(1))


def _ffn_body(with_attn_out, *refs):
    if with_attn_out:
        a_ref, wo_ref, mix_post_ref, x_ref, pre_ref, wg_ref, wu_ref, wd_ref, post_ref, o_ref = refs
        x = x_ref[...] + _rms(_dot(a_ref[...], wo_ref[...]), mix_post_ref[...])
    else:
        x_ref, pre_ref, wg_ref, wu_ref, wd_ref, post_ref, o_ref = refs
        x = x_ref[...]
    o_ref[...] = x
    xn = _rms(x, pre_ref[...]).astype(BF16)
    acc = None
    for c in range(FFN_NUM_CHUNKS):
        cols = slice(c * FFN_CHUNK, (c + 1) * FFN_CHUNK)
        g = _dot(xn, wg_ref[:, cols])
        u = _dot(xn, wu_ref[:, cols])
        h = (_silu(g) * u).astype(BF16)
        part = _dot(h, wd_ref[cols, :])
        acc = part if acc is None else acc + part
    o_ref[...] += _rms(acc, post_ref[...])


def _ffn(x2d, p, attn=None):
    tokens = x2d.shape[0]
    tm = FFN_TOKENS
    row = lambda w: pl.BlockSpec((tm, w), lambda i: (i, 0))
    attn_args, attn_specs = (), []
    if attn is not None:
        attn_args = attn
        attn_specs = [row(attn[0].shape[1]), _const_spec(attn[1].shape), _const_spec((1, D_MODEL))]
    return pl.pallas_call(
        functools.partial(_ffn_body, attn is not None),
        grid=(tokens // tm,),
        in_specs=attn_specs + [
            row(D_MODEL),
            _const_spec((1, D_MODEL)),
            _const_spec((D_MODEL, D_FF)),
            _const_spec((D_MODEL, D_FF)),
            _const_spec((D_FF, D_MODEL)),
            _const_spec((1, D_MODEL)),
        ],
        out_specs=row(D_MODEL),
        out_shape=jax.ShapeDtypeStruct((tokens, D_MODEL), F32),
        compiler_params=pltpu.CompilerParams(
            dimension_semantics=("parallel",), vmem_limit_bytes=VMEM_LIMIT_BYTES),
        name="ffn_attn_out" if attn is not None else "ffn",
    )(*attn_args, x2d, p["pre"], p["wg"], p["wu"], p["wd"], p["post"])


def _prep_ffn(pre_g, w_gate, w_up, w_down, post_g):
    return dict(
        pre=pre_g.reshape(1, D_MODEL), post=(FFN_RESIDUAL_WEIGHT * post_g).reshape(1, D_MODEL),
        wg=w_gate.astype(BF16), wu=w_up.astype(BF16), wd=w_down.astype(BF16))


def _conv_body(seq_len, xm_ref, xp_ref, xq_ref, pre_ref, w1_ref, b1_ref, wdw_ref, bdw_ref,
               lng_ref, lnb_ref, w2_ref, b2_ref, post_ref, o_ref, xn_ref, h_ref, c_ref):
    tm = CONV_TOKENS
    halo = CONV_HALO
    rows = tm + 2 * halo
    i = pl.program_id(1)
    pre = pre_ref[...]
    xn_ref[0:halo, :] = _rms(xp_ref[0], pre).astype(BF16)
    xn_ref[halo:halo + tm, :] = _rms(xm_ref[0], pre).astype(BF16)
    xn_ref[halo + tm:rows, :] = _rms(xq_ref[0], pre).astype(BF16)

    tok = i * tm - halo + lax.broadcasted_iota(jnp.int32, (rows, 1), 0)
    inside = jnp.logical_and(tok >= 0, tok < seq_len)
    lb = CONV_LANE_BLOCK
    for n in range(D_MODEL // lb):
        xn = xn_ref[...]
        val = _dot(xn, w1_ref[:, n * lb:(n + 1) * lb]) + b1_ref[:, n * lb:(n + 1) * lb]
        gate = (_dot(xn, w1_ref[:, D_MODEL + n * lb:D_MODEL + (n + 1) * lb])
                + b1_ref[:, D_MODEL + n * lb:D_MODEL + (n + 1) * lb])
        h = jnp.where(inside, val * _sigmoid(gate), 0.0)
        for j in range(lb // LANES):
            h_ref[n * (lb // LANES) + j] = h[:, j * LANES:(j + 1) * LANES]

    rb = CONV_ROW_BLOCK
    shift = halo - CONV_HALF
    for j in range(D_MODEL // LANES):
        lanes = slice(j * LANES, (j + 1) * LANES)
        for r in range(tm // rb):
            r0 = r * rb
            acc = jnp.broadcast_to(bdw_ref[:, lanes], (rb, LANES))
            for k in range(CONV_WIDTH):
                acc = acc + h_ref[j, r0 + shift + k:r0 + shift + k + rb, :] * wdw_ref[k:k + 1, lanes]
            c_ref[r0:r0 + rb, lanes] = acc

    c = c_ref[...]
    mu = jnp.mean(c, axis=-1, keepdims=True)
    cc = c - mu
    var = jnp.mean(cc * cc, axis=-1, keepdims=True)
    y = cc * lax.rsqrt(var + NORM_EPS) * lng_ref[...] + lnb_ref[...]
    z = _dot(_silu(y).astype(BF16), w2_ref[...]) + b2_ref[...]
    o_ref[0] = xm_ref[0] + _rms(z, post_ref[...])


def _conv(x, p):
    batch, seq_len, _ = x.shape
    tm = CONV_TOKENS
    halo = CONV_HALO
    per_tile = tm // halo
    last_halo_block = seq_len // halo - 1
    main = pl.BlockSpec((1, tm, D_MODEL), lambda b, i: (b, i, 0))
    prev = pl.BlockSpec((1, halo, D_MODEL),
                        lambda b, i: (b, jnp.maximum(i * per_tile - 1, 0), 0))
    nxt = pl.BlockSpec((1, halo, D_MODEL),
                       lambda b, i: (b, jnp.minimum((i + 1) * per_tile, last_halo_block), 0))
    return pl.pallas_call(
        functools.partial(_conv_body, seq_len),
        grid=(batch, seq_len // tm),
        in_specs=[
            main, prev, nxt,
            _const_spec((1, D_MODEL)),
            _const_spec((D_MODEL, 2 * D_MODEL)),
            _const_spec((1, 2 * D_MODEL)),
            _const_spec((CONV_WIDTH, D_MODEL)),
            _const_spec((1, D_MODEL)),
            _const_spec((1, D_MODEL)),
            _const_spec((1, D_MODEL)),
            _const_spec((D_MODEL, D_MODEL)),
            _const_spec((1, D_MODEL)),
            _const_spec((1, D_MODEL)),
        ],
        out_specs=main,
        out_shape=jax.ShapeDtypeStruct(x.shape, F32),
        scratch_shapes=[
            pltpu.VMEM((tm + 2 * halo, D_MODEL), BF16),
            pltpu.VMEM((D_MODEL // LANES, tm + 2 * halo, LANES), F32),
            pltpu.VMEM((tm, D_MODEL), F32),
        ],
        compiler_params=pltpu.CompilerParams(
            dimension_semantics=("parallel", "parallel"), vmem_limit_bytes=VMEM_LIMIT_BYTES),
        name="conv_module",
    )(x, x, x, p["pre"], p["w1"], p["b1"], p["wdw"], p["bdw"], p["lng"], p["lnb"],
      p["w2"], p["b2"], p["post"])


def _prep_conv(pre_g, w_pw1, b_pw1, w_dw, b_dw, ln_g, ln_b, w_pw2, b_pw2, post_g):
    row = lambda v: v.reshape(1, -1)
    return dict(pre=row(pre_g), w1=w_pw1.astype(BF16), b1=row(b_pw1), wdw=w_dw, bdw=row(b_dw),
                lng=row(ln_g), lnb=row(ln_b), w2=w_pw2.astype(BF16), b2=row(b_pw2),
                post=row(post_g))


def _mla_proj_body(x_ref, pre_ref, wdq_ref, qg_ref, wqa_ref, wqb_ref, wkv_ref, kvg_ref,
                   wuk_ref, wuv_ref, cos_ref, sin_ref, cos_t_ref, sin_t_ref,
                   qt_ref, k_ref, vt_ref):
    scale = (QK_NOPE_DIM + QK_ROPE_DIM) ** -0.5 * LOG2_E
    nt = (((1,), (1,)), ((), ()))
    xn = _rms(x_ref[0], pre_ref[...]).astype(BF16)
    cq = _rms(_dot(xn, wdq_ref[...]), qg_ref[...]).astype(BF16)
    kv = _dot(xn, wkv_ref[...])
    ckv = _rms(kv[:, :KV_LORA_RANK], kvg_ref[...]).astype(BF16)
    k_rope = (kv[:, KV_LORA_RANK:KV_LORA_RANK + LANES] * cos_ref[...]
              + kv[:, KV_LORA_RANK + LANES:] * sin_ref[...]).astype(BF16)
    k_nope = _dot(ckv, wuk_ref[...])
    for h in range(N_HEADS):
        lo = h * QK_PAD_DIM
        k_ref[0, :, lo:lo + LANES] = k_nope[:, h * QK_NOPE_DIM:(h + 1) * QK_NOPE_DIM].astype(BF16)
        k_ref[0, :, lo + LANES:lo + QK_PAD_DIM] = k_rope
    vt_ref[0] = lax.dot_general(wuv_ref[...], ckv, nt, preferred_element_type=F32).astype(BF16)
    cos_t = cos_t_ref[...]
    sin_t = sin_t_ref[...]
    qb_all = lax.dot_general(wqb_ref[...], cq, nt, preferred_element_type=F32)
    heads_per_dot = N_HEADS // 2
    for half in range(2):
        rows = heads_per_dot * QK_PAD_DIM
        qa_half = lax.dot_general(wqa_ref[half * rows:(half + 1) * rows, :], cq, nt,
                                  preferred_element_type=F32)
        for hh in range(heads_per_dot):
            h = half * heads_per_dot + hh
            lo = h * QK_PAD_DIM
            qa = qa_half[hh * QK_PAD_DIM:(hh + 1) * QK_PAD_DIM]
            qb = qb_all[h * LANES:(h + 1) * LANES]
            qt_ref[0, lo:lo + LANES, :] = (qa[:LANES] * scale).astype(BF16)
            qt_ref[0, lo + LANES:lo + QK_PAD_DIM, :] = (
                (qa[LANES:] * cos_t + qb * sin_t) * scale).astype(BF16)


def _attn_body(seq_len, qt_ref, k_ref, vt_ref, o_ref):
    tq = ATTN_Q_BLOCK
    n_kv = seq_len // ATTN_KV_BLOCK
    stages = [(qb, j) for qb in range(ATTN_Q_BLOCKS_PER_STEP) for j in range(n_kv)]

    def scores(stage):
        qb, j = stage
        return _dot(k_ref[0, j * ATTN_KV_BLOCK:(j + 1) * ATTN_KV_BLOCK, :],
                    qt_ref[0, :, qb * tq:(qb + 1) * tq])

    m = l = acc = None
    s_next = scores(stages[0])
    for idx, (qb, j) in enumerate(stages):
        s = s_next
        if idx + 1 < len(stages):
            s_next = scores(stages[idx + 1])
        vt = vt_ref[0, :, j * ATTN_KV_BLOCK:(j + 1) * ATTN_KV_BLOCK]
        s_max = jnp.max(s, axis=0, keepdims=True)
        if j == 0:
            m = s_max
            p = jnp.exp2(s - m)
            l = jnp.sum(p, axis=0, keepdims=True)
            acc = _dot(vt, p.astype(BF16))
        else:
            m_new = jnp.maximum(m, s_max)
            alpha = jnp.exp2(m - m_new)
            p = jnp.exp2(s - m_new)
            l = alpha * l + jnp.sum(p, axis=0, keepdims=True)
            acc = alpha * acc + _dot(vt, p.astype(BF16))
            m = m_new
        if j == n_kv - 1:
            o_ref[0, qb * tq:(qb + 1) * tq, :] = (acc / l).T.astype(BF16)


def _mla_attention(x, p, tabs):
    batch, seq_len, _ = x.shape
    tm = MLA_TOKENS
    qk_width = N_HEADS * QK_PAD_DIM
    v_width = N_HEADS * V_HEAD_DIM
    tile = lambda w: pl.BlockSpec((1, tm, w), lambda b, i: (b, i, 0))
    tile_t = lambda w: pl.BlockSpec((1, w, tm), lambda b, i: (b, 0, i))
    tab = pl.BlockSpec((tm, LANES), lambda b, i: (i, 0))
    tab_t = pl.BlockSpec((LANES, tm), lambda b, i: (0, i))
    qt, k, vt = pl.pallas_call(
        _mla_proj_body,
        grid=(batch, seq_len // tm),
        in_specs=[
            tile(D_MODEL),
            _const_spec((1, D_MODEL)),
            _const_spec((D_MODEL, Q_LORA_RANK)),
            _const_spec((1, Q_LORA_RANK)),
            _const_spec((qk_width, Q_LORA_RANK)),
            _const_spec((N_HEADS * LANES, Q_LORA_RANK)),
            _const_spec((D_MODEL, KV_LORA_RANK + 2 * LANES)),
            _const_spec((1, KV_LORA_RANK)),
            _const_spec((KV_LORA_RANK, N_HEADS * QK_NOPE_DIM)),
            _const_spec((v_width, KV_LORA_RANK)),
            tab, tab, tab_t, tab_t,
        ],
        out_specs=[tile_t(qk_width), tile(qk_width), tile_t(v_width)],
        out_shape=[
            jax.ShapeDtypeStruct((batch, qk_width, seq_len), BF16),
            jax.ShapeDtypeStruct((batch, seq_len, qk_width), BF16),
            jax.ShapeDtypeStruct((batch, v_width, seq_len), BF16),
        ],
        compiler_params=pltpu.CompilerParams(
            dimension_semantics=("parallel", "parallel"), vmem_limit_bytes=VMEM_LIMIT_BYTES),
        name="mla_proj",
    )(x, p["pre"], p["wdq"], p["qg"], p["wqa"], p["wqb"], p["wkv"], p["kvg"], p["wuk"],
      p["wuv"], *tabs)

    tq = ATTN_Q_BLOCK * ATTN_Q_BLOCKS_PER_STEP
    attn = pl.pallas_call(
        functools.partial(_attn_body, seq_len),
        grid=(batch, N_HEADS, seq_len // tq),
        in_specs=[
            pl.BlockSpec((1, QK_PAD_DIM, tq), lambda b, h, i: (b, h, i)),
            pl.BlockSpec((1, seq_len, QK_PAD_DIM), lambda b, h, i: (b, 0, h)),
            pl.BlockSpec((1, V_HEAD_DIM, seq_len), lambda b, h, i: (b, h, 0)),
        ],
        out_specs=pl.BlockSpec((1, tq, V_HEAD_DIM), lambda b, h, i: (b, i, h)),
        out_shape=jax.ShapeDtypeStruct((batch, seq_len, v_width), BF16),
        compiler_params=pltpu.CompilerParams(
            dimension_semantics=("parallel", "parallel", "parallel"),
            vmem_limit_bytes=VMEM_LIMIT_BYTES),
        name="mla_attention",
    )(qt, k, vt)

    return attn.reshape(batch * seq_len, v_width)


def _prep_mla(pre_g, w_dq, q_norm_g, w_uq, w_dkv, kv_norm_g, w_uk, w_uv, w_o, post_g):
    row = lambda v: v.reshape(1, -1)
    head_dim = QK_NOPE_DIM + QK_ROPE_DIM
    wq = w_uq.reshape(Q_LORA_RANK, N_HEADS, head_dim)
    q_x1 = wq[:, :, QK_NOPE_DIM:QK_NOPE_DIM + ROPE_HALF]
    q_x2 = wq[:, :, QK_NOPE_DIM + ROPE_HALF:]
    pad = lambda n: jnp.zeros((Q_LORA_RANK, N_HEADS, n), F32)
    wqa = jnp.concatenate([wq, pad(QK_PAD_DIM - head_dim)], axis=-1)
    wqb = jnp.concatenate([-q_x2, q_x1, pad(LANES - QK_ROPE_DIM)], axis=-1)
    k_x1 = w_dkv[:, KV_LORA_RANK:KV_LORA_RANK + ROPE_HALF]
    k_x2 = w_dkv[:, KV_LORA_RANK + ROPE_HALF:]
    kpad = jnp.zeros((D_MODEL, LANES - QK_ROPE_DIM), F32)
    wkv = jnp.concatenate([w_dkv[:, :KV_LORA_RANK], k_x1, k_x2, kpad, -k_x2, k_x1, kpad], axis=-1)
    return dict(
        pre=row(pre_g), wdq=w_dq.astype(BF16), qg=row(q_norm_g),
        wqa=wqa.reshape(Q_LORA_RANK, N_HEADS * QK_PAD_DIM).T.astype(BF16),
        wqb=wqb.reshape(Q_LORA_RANK, N_HEADS * LANES).T.astype(BF16),
        wkv=wkv.astype(BF16), kvg=row(kv_norm_g), wuk=w_uk.astype(BF16),
        wuv=w_uv.T.astype(BF16), wo=w_o.astype(BF16), post=row(post_g))


def _rope_tables(seq_len):
    pos = jnp.arange(seq_len, dtype=F32)
    inv_freq = ROPE_THETA ** (-jnp.arange(0, QK_ROPE_DIM, 2, dtype=F32) / QK_ROPE_DIM)
    ang = pos[:, None] * inv_freq[None, :]
    zeros = jnp.zeros((seq_len, LANES - QK_ROPE_DIM), F32)
    cos, sin = jnp.cos(ang), jnp.sin(ang)
    cos_tab = jnp.concatenate([cos, cos, zeros], axis=-1)
    sin_tab = jnp.concatenate([sin, sin, zeros], axis=-1)
    return cos_tab, sin_tab, cos_tab.T, sin_tab.T


def kernel(x_prompt, x_sample, l0_ffn1_pre_g, l0_ffn1_w_gate, l0_ffn1_w_up, l0_ffn1_w_down, l0_ffn1_post_g, l0_mix_pre_g, l0_conv_w_pw1, l0_conv_b_pw1, l0_conv_w_dw, l0_conv_b_dw, l0_conv_ln_g, l0_conv_ln_b, l0_conv_w_pw2, l0_conv_b_pw2, l0_mix_post_g, l0_ffn2_pre_g, l0_ffn2_w_gate, l0_ffn2_w_up, l0_ffn2_w_down, l0_ffn2_post_g, l1_ffn1_pre_g, l1_ffn1_w_gate, l1_ffn1_w_up, l1_ffn1_w_down, l1_ffn1_post_g, l1_mix_pre_g, l1_mla_w_dq, l1_mla_q_norm_g, l1_mla_w_uq, l1_mla_w_dkv, l1_mla_kv_norm_g, l1_mla_w_uk, l1_mla_w_uv, l1_mla_w_o, l1_mix_post_g, l1_ffn2_pre_g, l1_ffn2_w_gate, l1_ffn2_w_up, l1_ffn2_w_down, l1_ffn2_post_g):
    l0_ffn1 = _prep_ffn(l0_ffn1_pre_g, l0_ffn1_w_gate, l0_ffn1_w_up, l0_ffn1_w_down, l0_ffn1_post_g)
    l0_ffn2 = _prep_ffn(l0_ffn2_pre_g, l0_ffn2_w_gate, l0_ffn2_w_up, l0_ffn2_w_down, l0_ffn2_post_g)
    l1_ffn1 = _prep_ffn(l1_ffn1_pre_g, l1_ffn1_w_gate, l1_ffn1_w_up, l1_ffn1_w_down, l1_ffn1_post_g)
    l1_ffn2 = _prep_ffn(l1_ffn2_pre_g, l1_ffn2_w_gate, l1_ffn2_w_up, l1_ffn2_w_down, l1_ffn2_post_g)
    conv = _prep_conv(l0_mix_pre_g, l0_conv_w_pw1, l0_conv_b_pw1, l0_conv_w_dw, l0_conv_b_dw,
                      l0_conv_ln_g, l0_conv_ln_b, l0_conv_w_pw2, l0_conv_b_pw2, l0_mix_post_g)
    mla = _prep_mla(l1_mix_pre_g, l1_mla_w_dq, l1_mla_q_norm_g, l1_mla_w_uq, l1_mla_w_dkv,
                    l1_mla_kv_norm_g, l1_mla_w_uk, l1_mla_w_uv, l1_mla_w_o, l1_mix_post_g)

    def trunk(x):
        shape = x.shape
        flat = lambda a: a.reshape(-1, D_MODEL)
        tabs = _rope_tables(shape[1])
        x = _ffn(flat(x), l0_ffn1).reshape(shape)
        x = _conv(x, conv)
        x = _ffn(flat(x), l0_ffn2)
        x = _ffn(x, l1_ffn1).reshape(shape)
        attn = _mla_attention(x, mla, tabs)
        x = _ffn(flat(x), l1_ffn2, attn=(attn, mla["wo"], mla["post"])).reshape(shape)
        return x

    return (trunk(x_prompt), trunk(x_sample))
```

```python
import functools

import jax
import jax.numpy as jnp
from jax import lax
from jax.experimental import pallas as pl
from jax.experimental.pallas import tpu as pltpu

F32 = jnp.float32
BF16 = jnp.bfloat16

D_MODEL = 1024
D_FF = 2816
FFN_RESIDUAL_WEIGHT = 0.5
CONV_WIDTH = 31
CONV_HALF = CONV_WIDTH // 2
N_HEADS = 8
QK_NOPE_DIM = 128
QK_ROPE_DIM = 64
ROPE_HALF = QK_ROPE_DIM // 2
V_HEAD_DIM = 128
Q_LORA_RANK = 384
KV_LORA_RANK = 256
ROPE_THETA = 10000.0
NORM_EPS = 1e-6

LANES = 128
SUBLANES = 8
MXU_DIM = 256
QK_PAD_DIM = 256
VMEM_LIMIT_BYTES = 56 * 1024 * 1024

FFN_CHUNK = MXU_DIM
FFN_NUM_CHUNKS = D_FF // FFN_CHUNK
FFN_TOKENS = 512
CONV_TOKENS = 512
CONV_HALO = 16
CONV_ROW_BLOCK = 64
CONV_LANE_BLOCK = 256
MLA_TOKENS = 512
ATTN_Q_BLOCK = 512
ATTN_KV_BLOCK = 1024
ATTN_Q_BLOCKS_PER_STEP = 4
LOG2_E = 1.4426950408889634
NORM_BOUND_MARGIN = 1.02
MAX_SAFE_SHIFT = 40.0


def _rms(x, g):
    ms = jnp.mean(x * x, axis=-1, keepdims=True)
    return x * lax.rsqrt(ms + NORM_EPS) * g


def _silu(x):
    return x / (1.0 + jnp.exp(-x))


def _sigmoid(x):
    return 1.0 / (1.0 + jnp.exp(-x))


def _dot(a, b):
    return jnp.dot(a, b, preferred_element_type=F32)


def _const_spec(shape):
    zeros = (0,) * len(shape)
    return pl.BlockSpec(shape, lambda *_: zeros, pipeline_mode=pl.Buffered(1))


def _ffn_body(with_attn_out, *refs):
    if with_attn_out:
        a_ref, wo_ref, mix_post_ref, x_ref, pre_ref, wg_ref, wu_ref, wd_ref, post_ref, o_ref = refs
        x = x_ref[...] + _rms(_dot(a_ref[...], wo_ref[...]), mix_post_ref[...])
    else:
        x_ref, pre_ref, wg_ref, wu_ref, wd_ref, post_ref, o_ref = refs
        x = x_ref[...]
    o_ref[...] = x
    xn = _rms(x, pre_ref[...])
    acc = None
    for c in range(FFN_NUM_CHUNKS):
        cols = slice(c * FFN_CHUNK, (c + 1) * FFN_CHUNK)
        g = _dot(xn, wg_ref[:, cols])
        u = _dot(xn, wu_ref[:, cols])
        h = _silu(g) * u
        part = _dot(h, wd_ref[cols, :])
        acc = part if acc is None else acc + part
    o_ref[...] += _rms(acc, post_ref[...])


def _ffn(x2d, p, attn=None):
    tokens = x2d.shape[0]
    tm = FFN_TOKENS
    row = lambda w: pl.BlockSpec((tm, w), lambda i: (i, 0))
    attn_args, attn_specs = (), []
    if attn is not None:
        attn_args = attn
        attn_specs = [row(attn[0].shape[1]), _const_spec(attn[1].shape), _const_spec((1, D_MODEL))]
    return pl.pallas_call(
        functools.partial(_ffn_body, attn is not None),
        grid=(tokens // tm,),
        in_specs=attn_specs + [
            row(D_MODEL),
            _const_spec((1, D_MODEL)),
            _const_spec((D_MODEL, D_FF)),
            _const_spec((D_MODEL, D_FF)),
            _const_spec((D_FF, D_MODEL)),
            _const_spec((1, D_MODEL)),
        ],
        out_specs=row(D_MODEL),
        out_shape=jax.ShapeDtypeStruct((tokens, D_MODEL), F32),
        compiler_params=pltpu.CompilerParams(
            dimension_semantics=("parallel",), vmem_limit_bytes=VMEM_LIMIT_BYTES),
        name="ffn_attn_out" if attn is not None else "ffn",
    )(*attn_args, x2d, p["pre"], p["wg"], p["wu"], p["wd"], p["post"])


def _prep_ffn(pre_g, w_gate, w_up, w_down, post_g):
    return dict(
        pre=pre_g.reshape(1, D_MODEL), post=(FFN_RESIDUAL_WEIGHT * post_g).reshape(1, D_MODEL),
        wg=w_gate, wu=w_up, wd=w_down)


def _conv_body(seq_len, xm_ref, xp_ref, xq_ref, pre_ref, w1_ref, b1_ref, wdw_ref, bdw_ref,
               lng_ref, lnb_ref, w2_ref, b2_ref, post_ref, o_ref, xn_ref, h_ref, c_ref):
    tm = CONV_TOKENS
    halo = CONV_HALO
    rows = tm + 2 * halo
    i = pl.program_id(1)
    pre = pre_ref[...]
    xn_ref[0:halo, :] = _rms(xp_ref[0], pre).astype(BF16)
    xn_ref[halo:halo + tm, :] = _rms(xm_ref[0], pre).astype(BF16)
    xn_ref[halo + tm:rows, :] = _rms(xq_ref[0], pre).astype(BF16)

    tok = i * tm - halo + lax.broadcasted_iota(jnp.int32, (rows, 1), 0)
    inside = jnp.logical_and(tok >= 0, tok < seq_len)
    lb = CONV_LANE_BLOCK
    for n in range(D_MODEL // lb):
        xn = xn_ref[...]
        val = _dot(xn, w1_ref[:, n * lb:(n + 1) * lb]) + b1_ref[:, n * lb:(n + 1) * lb]
        gate = (_dot(xn, w1_ref[:, D_MODEL + n * lb:D_MODEL + (n + 1) * lb])
                + b1_ref[:, D_MODEL + n * lb:D_MODEL + (n + 1) * lb])
        h = jnp.where(inside, val * _sigmoid(gate), 0.0)
        for j in range(lb // LANES):
            h_ref[n * (lb // LANES) + j] = h[:, j * LANES:(j + 1) * LANES]

    rb = CONV_ROW_BLOCK
    shift = halo - CONV_HALF
    for j in range(D_MODEL // LANES):
        lanes = slice(j * LANES, (j + 1) * LANES)
        for r in range(tm // rb):
            r0 = r * rb
            acc = jnp.broadcast_to(bdw_ref[:, lanes], (rb, LANES))
            for k in range(CONV_WIDTH):
                acc = acc + h_ref[j, r0 + shift + k:r0 + shift + k + rb, :] * wdw_ref[k:k + 1, lanes]
            c_ref[r0:r0 + rb, lanes] = acc

    c = c_ref[...]
    mu = jnp.mean(c, axis=-1, keepdims=True)
    cc = c - mu
    var = jnp.mean(cc * cc, axis=-1, keepdims=True)
    y = cc * lax.rsqrt(var + NORM_EPS) * lng_ref[...] + lnb_ref[...]
    z = _dot(_silu(y).astype(BF16), w2_ref[...]) + b2_ref[...]
    o_ref[0] = xm_ref[0] + _rms(z, post_ref[...])


def _conv(x, p):
    batch, seq_len, _ = x.shape
    tm = CONV_TOKENS
    halo = CONV_HALO
    per_tile = tm // halo
    last_halo_block = seq_len // halo - 1
    main = pl.BlockSpec((1, tm, D_MODEL), lambda b, i: (b, i, 0))
    prev = pl.BlockSpec((1, halo, D_MODEL),
                        lambda b, i: (b, jnp.maximum(i * per_tile - 1, 0), 0))
    nxt = pl.BlockSpec((1, halo, D_MODEL),
                       lambda b, i: (b, jnp.minimum((i + 1) * per_tile, last_halo_block), 0))
    return pl.pallas_call(
        functools.partial(_conv_body, seq_len),
        grid=(batch, seq_len // tm),
        in_specs=[
            main, prev, nxt,
            _const_spec((1, D_MODEL)),
            _const_spec((D_MODEL, 2 * D_MODEL)),
            _const_spec((1, 2 * D_MODEL)),
            _const_spec((CONV_WIDTH, D_MODEL)),
            _const_spec((1, D_MODEL)),
            _const_spec((1, D_MODEL)),
            _const_spec((1, D_MODEL)),
            _const_spec((D_MODEL, D_MODEL)),
            _const_spec((1, D_MODEL)),
            _const_spec((1, D_MODEL)),
        ],
        out_specs=main,
        out_shape=jax.ShapeDtypeStruct(x.shape, F32),
        scratch_shapes=[
            pltpu.VMEM((tm + 2 * halo, D_MODEL), BF16),
            pltpu.VMEM((D_MODEL // LANES, tm + 2 * halo, LANES), F32),
            pltpu.VMEM((tm, D_MODEL), F32),
        ],
        compiler_params=pltpu.CompilerParams(
            dimension_semantics=("parallel", "parallel"), vmem_limit_bytes=VMEM_LIMIT_BYTES),
        name="conv_module",
    )(x, x, x, p["pre"], p["w1"], p["b1"], p["wdw"], p["bdw"], p["lng"], p["lnb"],
      p["w2"], p["b2"], p["post"])


def _prep_conv(pre_g, w_pw1, b_pw1, w_dw, b_dw, ln_g, ln_b, w_pw2, b_pw2, post_g):
    row = lambda v: v.reshape(1, -1)
    return dict(pre=row(pre_g), w1=w_pw1.astype(BF16), b1=row(b_pw1), wdw=w_dw, bdw=row(b_dw),
                lng=row(ln_g), lnb=row(ln_b), w2=w_pw2.astype(BF16), b2=row(b_pw2),
                post=row(post_g))


def _mla_proj_body(x_ref, pre_ref, wdq_ref, qg_ref, wqa_ref, wqb_ref, wkv_ref, kvg_ref,
                   wuk_ref, wuv_ref, cos_ref, sin_ref, cos_t_ref, sin_t_ref,
                   qt_ref, k_ref, vt_ref, kmax_ref):
    scale = (QK_NOPE_DIM + QK_ROPE_DIM) ** -0.5 * LOG2_E
    nt = (((1,), (1,)), ((), ()))
    xn = _rms(x_ref[0], pre_ref[...]).astype(BF16)
    cq = _rms(_dot(xn, wdq_ref[...]), qg_ref[...]).astype(BF16)
    kv = _dot(xn, wkv_ref[...])
    ckv = _rms(kv[:, :KV_LORA_RANK], kvg_ref[...]).astype(BF16)
    k_rope = (kv[:, KV_LORA_RANK:KV_LORA_RANK + LANES] * cos_ref[...]
              + kv[:, KV_LORA_RANK + LANES:] * sin_ref[...])
    k_rope_sq = jnp.sum(k_rope * k_rope, axis=1, keepdims=True)
    k_rope = k_rope.astype(BF16)
    k_nope = _dot(ckv, wuk_ref[...])
    k_sq_max = None
    for h in range(N_HEADS):
        lo = h * QK_PAD_DIM
        kn = k_nope[:, h * QK_NOPE_DIM:(h + 1) * QK_NOPE_DIM]
        k_sq = jnp.sum(kn * kn, axis=1, keepdims=True) + k_rope_sq
        k_sq_max = k_sq if k_sq_max is None else jnp.maximum(k_sq_max, k_sq)
        k_ref[0, :, lo:lo + LANES] = kn.astype(BF16)
        k_ref[0, :, lo + LANES:lo + QK_PAD_DIM] = k_rope
    kmax_ref[...] = jnp.broadcast_to(jnp.sqrt(jnp.max(k_sq_max, axis=0, keepdims=True)),
                                     kmax_ref.shape)
    vt_ref[0] = lax.dot_general(wuv_ref[...], ckv, nt, preferred_element_type=F32).astype(BF16)
    cos_t = cos_t_ref[...]
    sin_t = sin_t_ref[...]
    qb_all = lax.dot_general(wqb_ref[...], cq, nt, preferred_element_type=F32)
    norm_row = lax.broadcasted_iota(jnp.int32, cos_t.shape, 0) == LANES - 1
    heads_per_dot = N_HEADS // 2
    for half in range(2):
        rows = heads_per_dot * QK_PAD_DIM
        qa_half = lax.dot_general(wqa_ref[half * rows:(half + 1) * rows, :], cq, nt,
                                  preferred_element_type=F32)
        for hh in range(heads_per_dot):
            h = half * heads_per_dot + hh
            lo = h * QK_PAD_DIM
            qa = qa_half[hh * QK_PAD_DIM:(hh + 1) * QK_PAD_DIM]
            qb = qb_all[h * LANES:(h + 1) * LANES]
            q_nope = qa[:LANES] * scale
            q_rope = (qa[LANES:] * cos_t + qb * sin_t) * scale
            q_norm = jnp.sqrt(jnp.sum(q_nope * q_nope, axis=0, keepdims=True)
                              + jnp.sum(q_rope * q_rope, axis=0, keepdims=True))
            qt_ref[0, lo:lo + LANES, :] = q_nope.astype(BF16)
            qt_ref[0, lo + LANES:lo + QK_PAD_DIM, :] = jnp.where(norm_row, q_norm, q_rope).astype(BF16)


def _attn_body(seq_len, qt_ref, k_ref, vt_ref, kmax_ref, o_ref):
    tq = ATTN_Q_BLOCK
    n_kv = seq_len // ATTN_KV_BLOCK
    stages = [(qb, j) for qb in range(ATTN_Q_BLOCKS_PER_STEP) for j in range(n_kv)]

    def scores(stage):
        qb, j = stage
        return _dot(k_ref[0, j * ATTN_KV_BLOCK:(j + 1) * ATTN_KV_BLOCK, :],
                    qt_ref[0, :, qb * tq:(qb + 1) * tq])

    def attend(shift):
        m = l = acc = None
        s_next = scores(stages[0])
        for idx, (qb, j) in enumerate(stages):
            s = s_next
            if idx + 1 < len(stages):
                s_next = scores(stages[idx + 1])
            vt = vt_ref[0, :, j * ATTN_KV_BLOCK:(j + 1) * ATTN_KV_BLOCK]
            if shift is not None:
                p = jnp.exp2(s - shift[:, qb * tq:(qb + 1) * tq])
                p_sum = jnp.sum(p, axis=0, keepdims=True)
                pv = _dot(vt, p.astype(BF16))
                l, acc = (p_sum, pv) if j == 0 else (l + p_sum, acc + pv)
            elif j == 0:
                m = jnp.max(s, axis=0, keepdims=True)
                p = jnp.exp2(s - m)
                l = jnp.sum(p, axis=0, keepdims=True)
                acc = _dot(vt, p.astype(BF16))
            else:
                m_new = jnp.maximum(m, jnp.max(s, axis=0, keepdims=True))
                alpha = jnp.exp2(m - m_new)
                p = jnp.exp2(s - m_new)
                l = alpha * l + jnp.sum(p, axis=0, keepdims=True)
                acc = alpha * acc + _dot(vt, p.astype(BF16))
                m = m_new
            if j == n_kv - 1:
                o_ref[0, qb * tq:(qb + 1) * tq, :] = (acc / l).T.astype(BF16)

    bound = ((NORM_BOUND_MARGIN * jnp.max(kmax_ref[...]))
             * qt_ref[0, QK_PAD_DIM - 1:QK_PAD_DIM, :].astype(F32))
    bound_is_safe = jnp.max(bound) <= MAX_SAFE_SHIFT

    @pl.when(bound_is_safe)
    def _():
        attend(bound)

    @pl.when(jnp.logical_not(bound_is_safe))
    def _():
        attend(None)


def _mla_attention(x, p, tabs):
    batch, seq_len, _ = x.shape
    tm = MLA_TOKENS
    qk_width = N_HEADS * QK_PAD_DIM
    v_width = N_HEADS * V_HEAD_DIM
    tile = lambda w: pl.BlockSpec((1, tm, w), lambda b, i: (b, i, 0))
    tile_t = lambda w: pl.BlockSpec((1, w, tm), lambda b, i: (b, 0, i))
    tab = pl.BlockSpec((tm, LANES), lambda b, i: (i, 0))
    tab_t = pl.BlockSpec((LANES, tm), lambda b, i: (0, i))
    qt, k, vt, kmax = pl.pallas_call(
        _mla_proj_body,
        grid=(batch, seq_len // tm),
        in_specs=[
            tile(D_MODEL),
            _const_spec((1, D_MODEL)),
            _const_spec((D_MODEL, Q_LORA_RANK)),
            _const_spec((1, Q_LORA_RANK)),
            _const_spec((qk_width, Q_LORA_RANK)),
            _const_spec((N_HEADS * LANES, Q_LORA_RANK)),
            _const_spec((D_MODEL, KV_LORA_RANK + 2 * LANES)),
            _const_spec((1, KV_LORA_RANK)),
            _const_spec((KV_LORA_RANK, N_HEADS * QK_NOPE_DIM)),
            _const_spec((v_width, KV_LORA_RANK)),
            tab, tab, tab_t, tab_t,
        ],
        out_specs=[tile_t(qk_width), tile(qk_width), tile_t(v_width),
                   pl.BlockSpec((1, 1, SUBLANES, LANES), lambda b, i: (b, i, 0, 0))],
        out_shape=[
            jax.ShapeDtypeStruct((batch, qk_width, seq_len), BF16),
            jax.ShapeDtypeStruct((batch, seq_len, qk_width), BF16),
            jax.ShapeDtypeStruct((batch, v_width, seq_len), BF16),
            jax.ShapeDtypeStruct((batch, seq_len // tm, SUBLANES, LANES), F32),
        ],
        compiler_params=pltpu.CompilerParams(
            dimension_semantics=("parallel", "parallel"), vmem_limit_bytes=VMEM_LIMIT_BYTES),
        name="mla_proj",
    )(x, p["pre"], p["wdq"], p["qg"], p["wqa"], p["wqb"], p["wkv"], p["kvg"], p["wuk"],
      p["wuv"], *tabs)

    tq = ATTN_Q_BLOCK * ATTN_Q_BLOCKS_PER_STEP
    attn = pl.pallas_call(
        functools.partial(_attn_body, seq_len),
        grid=(batch, N_HEADS, seq_len // tq),
        in_specs=[
            pl.BlockSpec((1, QK_PAD_DIM, tq), lambda b, h, i: (b, h, i)),
            pl.BlockSpec((1, seq_len, QK_PAD_DIM), lambda b, h, i: (b, 0, h)),
            pl.BlockSpec((1, V_HEAD_DIM, seq_len), lambda b, h, i: (b, h, 0)),
            pl.BlockSpec((1, seq_len // tm, SUBLANES, LANES), lambda b, h, i: (b, 0, 0, 0)),
        ],
        out_specs=pl.BlockSpec((1, tq, V_HEAD_DIM), lambda b, h, i: (b, i, h)),
        out_shape=jax.ShapeDtypeStruct((batch, seq_len, v_width), BF16),
        compiler_params=pltpu.CompilerParams(
            dimension_semantics=("parallel", "parallel", "parallel"),
            vmem_limit_bytes=VMEM_LIMIT_BYTES),
        name="mla_attention",
    )(qt, k, vt, kmax)

    return attn.reshape(batch * seq_len, v_width)


def _prep_mla(pre_g, w_dq, q_norm_g, w_uq, w_dkv, kv_norm_g, w_uk, w_uv, w_o, post_g):
    row = lambda v: v.reshape(1, -1)
    head_dim = QK_NOPE_DIM + QK_ROPE_DIM
    wq = w_uq.reshape(Q_LORA_RANK, N_HEADS, head_dim)
    q_x1 = wq[:, :, QK_NOPE_DIM:QK_NOPE_DIM + ROPE_HALF]
    q_x2 = wq[:, :, QK_NOPE_DIM + ROPE_HALF:]
    pad = lambda n: jnp.zeros((Q_LORA_RANK, N_HEADS, n), F32)
    wqa = jnp.concatenate([wq, pad(QK_PAD_DIM - head_dim)], axis=-1)
    wqb = jnp.concatenate([-q_x2, q_x1, pad(LANES - QK_ROPE_DIM)], axis=-1)
    k_x1 = w_dkv[:, KV_LORA_RANK:KV_LORA_RANK + ROPE_HALF]
    k_x2 = w_dkv[:, KV_LORA_RANK + ROPE_HALF:]
    kpad = jnp.zeros((D_MODEL, LANES - QK_ROPE_DIM), F32)
    wkv = jnp.concatenate([w_dkv[:, :KV_LORA_RANK], k_x1, k_x2, kpad, -k_x2, k_x1, kpad], axis=-1)
    return dict(
        pre=row(pre_g), wdq=w_dq.astype(BF16), qg=row(q_norm_g),
        wqa=wqa.reshape(Q_LORA_RANK, N_HEADS * QK_PAD_DIM).T.astype(BF16),
        wqb=wqb.reshape(Q_LORA_RANK, N_HEADS * LANES).T.astype(BF16),
        wkv=wkv.astype(BF16), kvg=row(kv_norm_g), wuk=w_uk.astype(BF16),
        wuv=w_uv.T.astype(BF16), wo=w_o.astype(BF16), post=row(post_g))


def _rope_tables(seq_len):
    pos = jnp.arange(seq_len, dtype=F32)
    inv_freq = ROPE_THETA ** (-jnp.arange(0, QK_ROPE_DIM, 2, dtype=F32) / QK_ROPE_DIM)
    ang = pos[:, None] * inv_freq[None, :]
    zeros = jnp.zeros((seq_len, LANES - QK_ROPE_DIM), F32)
    cos, sin = jnp.cos(ang), jnp.sin(ang)
    cos_tab = jnp.concatenate([cos, cos, zeros], axis=-1)
    sin_tab = jnp.concatenate([sin, sin, zeros], axis=-1)
    return cos_tab, sin_tab, cos_tab.T, sin_tab.T


def kernel(x_prompt, x_sample, l0_ffn1_pre_g, l0_ffn1_w_gate, l0_ffn1_w_up, l0_ffn1_w_down, l0_ffn1_post_g, l0_mix_pre_g, l0_conv_w_pw1, l0_conv_b_pw1, l0_conv_w_dw, l0_conv_b_dw, l0_conv_ln_g, l0_conv_ln_b, l0_conv_w_pw2, l0_conv_b_pw2, l0_mix_post_g, l0_ffn2_pre_g, l0_ffn2_w_gate, l0_ffn2_w_up, l0_ffn2_w_down, l0_ffn2_post_g, l1_ffn1_pre_g, l1_ffn1_w_gate, l1_ffn1_w_up, l1_ffn1_w_down, l1_ffn1_post_g, l1_mix_pre_g, l1_mla_w_dq, l1_mla_q_norm_g, l1_mla_w_uq, l1_mla_w_dkv, l1_mla_kv_norm_g, l1_mla_w_uk, l1_mla_w_uv, l1_mla_w_o, l1_mix_post_g, l1_ffn2_pre_g, l1_ffn2_w_gate, l1_ffn2_w_up, l1_ffn2_w_down, l1_ffn2_post_g):
    l0_ffn1 = _prep_ffn(l0_ffn1_pre_g, l0_ffn1_w_gate, l0_ffn1_w_up, l0_ffn1_w_down, l0_ffn1_post_g)
    l0_ffn2 = _prep_ffn(l0_ffn2_pre_g, l0_ffn2_w_gate, l0_ffn2_w_up, l0_ffn2_w_down, l0_ffn2_post_g)
    l1_ffn1 = _prep_ffn(l1_ffn1_pre_g, l1_ffn1_w_gate, l1_ffn1_w_up, l1_ffn1_w_down, l1_ffn1_post_g)
    l1_ffn2 = _prep_ffn(l1_ffn2_pre_g, l1_ffn2_w_gate, l1_ffn2_w_up, l1_ffn2_w_down, l1_ffn2_post_g)
    conv = _prep_conv(l0_mix_pre_g, l0_conv_w_pw1, l0_conv_b_pw1, l0_conv_w_dw, l0_conv_b_dw,
                      l0_conv_ln_g, l0_conv_ln_b, l0_conv_w_pw2, l0_conv_b_pw2, l0_mix_post_g)
    mla = _prep_mla(l1_mix_pre_g, l1_mla_w_dq, l1_mla_q_norm_g, l1_mla_w_uq, l1_mla_w_dkv,
                    l1_mla_kv_norm_g, l1_mla_w_uk, l1_mla_w_uv, l1_mla_w_o, l1_mix_post_g)

    def trunk(x):
        shape = x.shape
        flat = lambda a: a.reshape(-1, D_MODEL)
        tabs = _rope_tables(shape[1])
        x = _ffn(flat(x), l0_ffn1).reshape(shape)
        x = _conv(x, conv)
        x = _ffn(flat(x), l0_ffn2)
        x = _ffn(x, l1_ffn1).reshape(shape)
        attn = _mla_attention(x, mla, tabs)
        x = _ffn(flat(x), l1_ffn2, attn=(attn, mla["wo"], mla["post"])).reshape(shape)
        return x

    return (trunk(x_prompt), trunk(x_sample))
```

```python
import functools

import jax
import jax.numpy as jnp
from jax import lax
from jax.experimental import pallas as pl
from jax.experimental.pallas import tpu as pltpu

F32 = jnp.float32
BF16 = jnp.bfloat16

D_MODEL = 1024
D_FF = 2816
FFN_RESIDUAL_WEIGHT = 0.5
CONV_WIDTH = 31
CONV_HALF = CONV_WIDTH // 2
N_HEADS = 8
QK_NOPE_DIM = 128
QK_ROPE_DIM = 64
ROPE_HALF = QK_ROPE_DIM // 2
V_HEAD_DIM = 128
Q_LORA_RANK = 384
KV_LORA_RANK = 256
ROPE_THETA = 10000.0
NORM_EPS = 1e-6

LANES = 128
SUBLANES = 8
MXU_DIM = 256
QK_PAD_DIM = 256
VMEM_LIMIT_BYTES = 56 * 1024 * 1024

FFN_CHUNK = MXU_DIM
FFN_NUM_CHUNKS = D_FF // FFN_CHUNK
FFN_TOKENS = 512
CONV_TOKENS = 512
CONV_HALO = 16
CONV_ROW_BLOCK = 64
CONV_LANE_BLOCK = 256
MLA_TOKENS = 512
ATTN_Q_BLOCK = 512
ATTN_KV_BLOCK = 1024
ATTN_Q_BLOCKS_PER_STEP = 4
LOG2_E = 1.4426950408889634
NORM_BOUND_MARGIN = 1.02
MAX_SAFE_SHIFT = 40.0


def _rms(x, g):
    ms = jnp.mean(x * x, axis=-1, keepdims=True)
    return x * lax.rsqrt(ms + NORM_EPS) * g


def _silu(x):
    return x / (1.0 + jnp.exp(-x))


def _sigmoid(x):
    return 1.0 / (1.0 + jnp.exp(-x))


def _dot(a, b):
    return jnp.dot(a, b, preferred_element_type=F32)


def _const_spec(shape):
    zeros = (0,) * len(shape)
    return pl.BlockSpec(shape, lambda *_: zeros, pipeline_mode=pl.Buffered(1))


def _ffn_body(with_attn_out, *refs):
    if with_attn_out:
        a_ref, wo_ref, mix_post_ref, x_ref, pre_ref, wg_ref, wu_ref, wd_ref, post_ref, o_ref = refs
        x = x_ref[...] + _rms(_dot(a_ref[...], wo_ref[...]), mix_post_ref[...])
    else:
        x_ref, pre_ref, wg_ref, wu_ref, wd_ref, post_ref, o_ref = refs
        x = x_ref[...]
    o_ref[...] = x
    xn = _rms(x, pre_ref[...])
    acc = None
    for c in range(FFN_NUM_CHUNKS):
        cols = slice(c * FFN_CHUNK, (c + 1) * FFN_CHUNK)
        g = _dot(xn, wg_ref[:, cols])
        u = _dot(xn, wu_ref[:, cols])
        h = _silu(g) * u
        part = _dot(h, wd_ref[cols, :])
        acc = part if acc is None else acc + part
    o_ref[...] += _rms(acc, post_ref[...])


def _ffn(x2d, p, attn=None):
    tokens = x2d.shape[0]
    tm = FFN_TOKENS
    row = lambda w: pl.BlockSpec((tm, w), lambda i: (i, 0))
    attn_args, attn_specs = (), []
    if attn is not None:
        attn_args = attn
        attn_specs = [row(attn[0].shape[1]), _const_spec(attn[1].shape), _const_spec((1, D_MODEL))]
    return pl.pallas_call(
        functools.partial(_ffn_body, attn is not None),
        grid=(tokens // tm,),
        in_specs=attn_specs + [
            row(D_MODEL),
            _const_spec((1, D_MODEL)),
            _const_spec((D_MODEL, D_FF)),
            _const_spec((D_MODEL, D_FF)),
            _const_spec((D_FF, D_MODEL)),
            _const_spec((1, D_MODEL)),
        ],
        out_specs=row(D_MODEL),
        out_shape=jax.ShapeDtypeStruct((tokens, D_MODEL), F32),
        compiler_params=pltpu.CompilerParams(
            dimension_semantics=("parallel",), vmem_limit_bytes=VMEM_LIMIT_BYTES),
        name="ffn_attn_out" if attn is not None else "ffn",
    )(*attn_args, x2d, p["pre"], p["wg"], p["wu"], p["wd"], p["post"])


def _prep_ffn(pre_g, w_gate, w_up, w_down, post_g):
    return dict(
        pre=pre_g.reshape(1, D_MODEL), post=(FFN_RESIDUAL_WEIGHT * post_g).reshape(1, D_MODEL),
        wg=w_gate, wu=w_up, wd=w_down)


def _conv_body(seq_len, xm_ref, xp_ref, xq_ref, pre_ref, w1_ref, b1_ref, wdw_ref, bdw_ref,
               lng_ref, lnb_ref, w2_ref, b2_ref, post_ref, o_ref, xn_ref, h_ref, c_ref):
    tm = CONV_TOKENS
    halo = CONV_HALO
    rows = tm + 2 * halo
    i = pl.program_id(1)
    pre = pre_ref[...]
    xn_ref[0:halo, :] = _rms(xp_ref[0], pre).astype(BF16)
    xn_ref[halo:halo + tm, :] = _rms(xm_ref[0], pre).astype(BF16)
    xn_ref[halo + tm:rows, :] = _rms(xq_ref[0], pre).astype(BF16)

    tok = i * tm - halo + lax.broadcasted_iota(jnp.int32, (rows, 1), 0)
    inside = jnp.logical_and(tok >= 0, tok < seq_len)
    lb = CONV_LANE_BLOCK
    for n in range(D_MODEL // lb):
        xn = xn_ref[...]
        val = _dot(xn, w1_ref[:, n * lb:(n + 1) * lb]) + b1_ref[:, n * lb:(n + 1) * lb]
        gate = (_dot(xn, w1_ref[:, D_MODEL + n * lb:D_MODEL + (n + 1) * lb])
                + b1_ref[:, D_MODEL + n * lb:D_MODEL + (n + 1) * lb])
        h = jnp.where(inside, val * _sigmoid(gate), 0.0)
        for j in range(lb // LANES):
            h_ref[n * (lb // LANES) + j] = h[:, j * LANES:(j + 1) * LANES]

    rb = CONV_ROW_BLOCK
    shift = halo - CONV_HALF
    for j in range(D_MODEL // LANES):
        lanes = slice(j * LANES, (j + 1) * LANES)
        for r in range(tm // rb):
            r0 = r * rb
            acc = jnp.broadcast_to(bdw_ref[:, lanes], (rb, LANES))
            for k in range(CONV_WIDTH):
                acc = acc + h_ref[j, r0 + shift + k:r0 + shift + k + rb, :] * wdw_ref[k:k + 1, lanes]
            c_ref[r0:r0 + rb, lanes] = acc

    c = c_ref[...]
    mu = jnp.mean(c, axis=-1, keepdims=True)
    cc = c - mu
    var = jnp.mean(cc * cc, axis=-1, keepdims=True)
    y = cc * lax.rsqrt(var + NORM_EPS) * lng_ref[...] + lnb_ref[...]
    z = _dot(_silu(y).astype(BF16), w2_ref[...]) + b2_ref[...]
    o_ref[0] = xm_ref[0] + _rms(z, post_ref[...])


def _conv(x, p):
    batch, seq_len, _ = x.shape
    tm = CONV_TOKENS
    halo = CONV_HALO
    per_tile = tm // halo
    last_halo_block = seq_len // halo - 1
    main = pl.BlockSpec((1, tm, D_MODEL), lambda b, i: (b, i, 0))
    prev = pl.BlockSpec((1, halo, D_MODEL),
                        lambda b, i: (b, jnp.maximum(i * per_tile - 1, 0), 0))
    nxt = pl.BlockSpec((1, halo, D_MODEL),
                       lambda b, i: (b, jnp.minimum((i + 1) * per_tile, last_halo_block), 0))
    return pl.pallas_call(
        functools.partial(_conv_body, seq_len),
        grid=(batch, seq_len // tm),
        in_specs=[
            main, prev, nxt,
            _const_spec((1, D_MODEL)),
            _const_spec((D_MODEL, 2 * D_MODEL)),
            _const_spec((1, 2 * D_MODEL)),
            _const_spec((CONV_WIDTH, D_MODEL)),
            _const_spec((1, D_MODEL)),
            _const_spec((1, D_MODEL)),
            _const_spec((1, D_MODEL)),
            _const_spec((D_MODEL, D_MODEL)),
            _const_spec((1, D_MODEL)),
            _const_spec((1, D_MODEL)),
        ],
        out_specs=main,
        out_shape=jax.ShapeDtypeStruct(x.shape, F32),
        scratch_shapes=[
            pltpu.VMEM((tm + 2 * halo, D_MODEL), BF16),
            pltpu.VMEM((D_MODEL // LANES, tm + 2 * halo, LANES), F32),
            pltpu.VMEM((tm, D_MODEL), F32),
        ],
        compiler_params=pltpu.CompilerParams(
            dimension_semantics=("parallel", "parallel"), vmem_limit_bytes=VMEM_LIMIT_BYTES),
        name="conv_module",
    )(x, x, x, p["pre"], p["w1"], p["b1"], p["wdw"], p["bdw"], p["lng"], p["lnb"],
      p["w2"], p["b2"], p["post"])


def _prep_conv(pre_g, w_pw1, b_pw1, w_dw, b_dw, ln_g, ln_b, w_pw2, b_pw2, post_g):
    row = lambda v: v.reshape(1, -1)
    return dict(pre=row(pre_g), w1=w_pw1.astype(BF16), b1=row(b_pw1), wdw=w_dw, bdw=row(b_dw),
                lng=row(ln_g), lnb=row(ln_b), w2=w_pw2.astype(BF16), b2=row(b_pw2),
                post=row(post_g))


def _mla_proj_body(x_ref, pre_ref, wdq_ref, qg_ref, wqn_ref, wqr_ref, wqb_ref, wkv_ref, kvg_ref,
                   wuk_ref, wuv_ref, cos_ref, sin_ref, cos_t_ref, sin_t_ref,
                   qt_ref, k_ref, vt_ref, kmax_ref):
    scale = (QK_NOPE_DIM + QK_ROPE_DIM) ** -0.5 * LOG2_E
    nt = (((1,), (1,)), ((), ()))
    xn = _rms(x_ref[0], pre_ref[...]).astype(BF16)
    cq = _rms(_dot(xn, wdq_ref[...]), qg_ref[...]).astype(BF16)
    kv = _dot(xn, wkv_ref[...])
    ckv = _rms(kv[:, :KV_LORA_RANK], kvg_ref[...]).astype(BF16)
    k_rope = (kv[:, KV_LORA_RANK:KV_LORA_RANK + LANES] * cos_ref[...]
              + kv[:, KV_LORA_RANK + LANES:] * sin_ref[...])
    k_rope_sq = jnp.sum(k_rope * k_rope, axis=1, keepdims=True)
    k_rope = k_rope.astype(BF16)
    k_nope = _dot(ckv, wuk_ref[...])
    k_sq_max = None
    for h in range(N_HEADS):
        lo = h * QK_PAD_DIM
        kn = k_nope[:, h * QK_NOPE_DIM:(h + 1) * QK_NOPE_DIM]
        k_sq = jnp.sum(kn * kn, axis=1, keepdims=True) + k_rope_sq
        k_sq_max = k_sq if k_sq_max is None else jnp.maximum(k_sq_max, k_sq)
        k_ref[0, :, lo:lo + LANES] = kn.astype(BF16)
        k_ref[0, :, lo + LANES:lo + QK_PAD_DIM] = k_rope
    kmax_ref[...] = jnp.broadcast_to(jnp.sqrt(jnp.max(k_sq_max, axis=0, keepdims=True)),
                                     kmax_ref.shape)
    vt_ref[0] = lax.dot_general(wuv_ref[...], ckv, nt, preferred_element_type=F32).astype(BF16)
    cos_t = cos_t_ref[...]
    sin_t = sin_t_ref[...]
    q_nope_all = lax.dot_general(wqn_ref[...], cq, nt, preferred_element_type=F32) * scale
    q_rope_all = lax.dot_general(wqr_ref[...], cq, nt, preferred_element_type=F32)
    q_rot_all = lax.dot_general(wqb_ref[...], cq, nt, preferred_element_type=F32)
    last_row = lax.broadcasted_iota(jnp.int32, cos_t.shape, 0) == QK_ROPE_DIM - 1
    for h in range(N_HEADS):
        lo = h * QK_PAD_DIM
        rope_rows = slice(h * QK_ROPE_DIM, (h + 1) * QK_ROPE_DIM)
        q_nope = q_nope_all[h * QK_NOPE_DIM:(h + 1) * QK_NOPE_DIM]
        q_rope = (q_rope_all[rope_rows] * cos_t + q_rot_all[rope_rows] * sin_t) * scale
        q_norm = jnp.sqrt(jnp.sum(q_nope * q_nope, axis=0, keepdims=True)
                          + jnp.sum(q_rope * q_rope, axis=0, keepdims=True))
        qt_ref[0, lo:lo + QK_NOPE_DIM, :] = q_nope.astype(BF16)
        qt_ref[0, lo + QK_NOPE_DIM:lo + QK_NOPE_DIM + QK_ROPE_DIM, :] = q_rope.astype(BF16)
        qt_ref[0, lo + QK_NOPE_DIM + QK_ROPE_DIM:lo + QK_PAD_DIM, :] = jnp.where(
            last_row, q_norm, 0.0).astype(BF16)


def _attn_body(seq_len, qt_ref, k_ref, vt_ref, kmax_ref, o_ref):
    tq = ATTN_Q_BLOCK
    n_kv = seq_len // ATTN_KV_BLOCK
    stages = [(qb, j) for qb in range(ATTN_Q_BLOCKS_PER_STEP) for j in range(n_kv)]

    def scores(stage):
        qb, j = stage
        return _dot(k_ref[0, j * ATTN_KV_BLOCK:(j + 1) * ATTN_KV_BLOCK, :],
                    qt_ref[0, :, qb * tq:(qb + 1) * tq])

    def attend(shift):
        m = l = acc = None
        s_next = scores(stages[0])
        for idx, (qb, j) in enumerate(stages):
            s = s_next
            if idx + 1 < len(stages):
                s_next = scores(stages[idx + 1])
            vt = vt_ref[0, :, j * ATTN_KV_BLOCK:(j + 1) * ATTN_KV_BLOCK]
            if shift is not None:
                p = jnp.exp2(s - shift[:, qb * tq:(qb + 1) * tq])
                p_sum = jnp.sum(p, axis=0, keepdims=True)
                pv = _dot(vt, p.astype(BF16))
                l, acc = (p_sum, pv) if j == 0 else (l + p_sum, acc + pv)
            elif j == 0:
                m = jnp.max(s, axis=0, keepdims=True)
                p = jnp.exp2(s - m)
                l = jnp.sum(p, axis=0, keepdims=True)
                acc = _dot(vt, p.astype(BF16))
            else:
                m_new = jnp.maximum(m, jnp.max(s, axis=0, keepdims=True))
                alpha = jnp.exp2(m - m_new)
                p = jnp.exp2(s - m_new)
                l = alpha * l + jnp.sum(p, axis=0, keepdims=True)
                acc = alpha * acc + _dot(vt, p.astype(BF16))
                m = m_new
            if j == n_kv - 1:
                o_ref[0, qb * tq:(qb + 1) * tq, :] = (acc / l).T.astype(BF16)

    q_norm = qt_ref[0, QK_PAD_DIM - 1:QK_PAD_DIM, :].astype(F32)
    k_norm_max = NORM_BOUND_MARGIN * jnp.max(kmax_ref[...])
    bound = k_norm_max * q_norm
    bound_is_safe = k_norm_max * jnp.max(q_norm) <= MAX_SAFE_SHIFT

    @pl.when(bound_is_safe)
    def _():
        attend(bound)

    @pl.when(jnp.logical_not(bound_is_safe))
    def _():
        attend(None)


def _mla_attention(x, p, tabs):
    batch, seq_len, _ = x.shape
    tm = MLA_TOKENS
    qk_width = N_HEADS * QK_PAD_DIM
    v_width = N_HEADS * V_HEAD_DIM
    tile = lambda w: pl.BlockSpec((1, tm, w), lambda b, i: (b, i, 0))
    tile_t = lambda w: pl.BlockSpec((1, w, tm), lambda b, i: (b, 0, i))
    tab = pl.BlockSpec((tm, LANES), lambda b, i: (i, 0))
    tab_t = pl.BlockSpec((QK_ROPE_DIM, tm), lambda b, i: (0, i))
    qt, k, vt, kmax = pl.pallas_call(
        _mla_proj_body,
        grid=(batch, seq_len // tm),
        in_specs=[
            tile(D_MODEL),
            _const_spec((1, D_MODEL)),
            _const_spec((D_MODEL, Q_LORA_RANK)),
            _const_spec((1, Q_LORA_RANK)),
            _const_spec((N_HEADS * QK_NOPE_DIM, Q_LORA_RANK)),
            _const_spec((N_HEADS * QK_ROPE_DIM, Q_LORA_RANK)),
            _const_spec((N_HEADS * QK_ROPE_DIM, Q_LORA_RANK)),
            _const_spec((D_MODEL, KV_LORA_RANK + 2 * LANES)),
            _const_spec((1, KV_LORA_RANK)),
            _const_spec((KV_LORA_RANK, N_HEADS * QK_NOPE_DIM)),
            _const_spec((v_width, KV_LORA_RANK)),
            tab, tab, tab_t, tab_t,
        ],
        out_specs=[tile_t(qk_width), tile(qk_width), tile_t(v_width),
                   pl.BlockSpec((1, 1, SUBLANES, LANES), lambda b, i: (b, i, 0, 0))],
        out_shape=[
            jax.ShapeDtypeStruct((batch, qk_width, seq_len), BF16),
            jax.ShapeDtypeStruct((batch, seq_len, qk_width), BF16),
            jax.ShapeDtypeStruct((batch, v_width, seq_len), BF16),
            jax.ShapeDtypeStruct((batch, seq_len // tm, SUBLANES, LANES), F32),
        ],
        compiler_params=pltpu.CompilerParams(
            dimension_semantics=("parallel", "parallel"), vmem_limit_bytes=VMEM_LIMIT_BYTES),
        name="mla_proj",
    )(x, p["pre"], p["wdq"], p["qg"], p["wqn"], p["wqr"], p["wqb"], p["wkv"], p["kvg"], p["wuk"],
      p["wuv"], *tabs)

    tq = ATTN_Q_BLOCK * ATTN_Q_BLOCKS_PER_STEP
    attn = pl.pallas_call(
        functools.partial(_attn_body, seq_len),
        grid=(batch, N_HEADS, seq_len // tq),
        in_specs=[
            pl.BlockSpec((1, QK_PAD_DIM, tq), lambda b, h, i: (b, h, i)),
            pl.BlockSpec((1, seq_len, QK_PAD_DIM), lambda b, h, i: (b, 0, h)),
            pl.BlockSpec((1, V_HEAD_DIM, seq_len), lambda b, h, i: (b, h, 0)),
            pl.BlockSpec((1, seq_len // tm, SUBLANES, LANES), lambda b, h, i: (b, 0, 0, 0)),
        ],
        out_specs=pl.BlockSpec((1, tq, V_HEAD_DIM), lambda b, h, i: (b, i, h)),
        out_shape=jax.ShapeDtypeStruct((batch, seq_len, v_width), BF16),
        compiler_params=pltpu.CompilerParams(
            dimension_semantics=("parallel", "parallel", "parallel"),
            vmem_limit_bytes=VMEM_LIMIT_BYTES),
        name="mla_attention",
    )(qt, k, vt, kmax)

    return attn.reshape(batch * seq_len, v_width)


def _prep_mla(pre_g, w_dq, q_norm_g, w_uq, w_dkv, kv_norm_g, w_uk, w_uv, w_o, post_g):
    row = lambda v: v.reshape(1, -1)
    head_dim = QK_NOPE_DIM + QK_ROPE_DIM
    wq = w_uq.reshape(Q_LORA_RANK, N_HEADS, head_dim)
    q_x1 = wq[:, :, QK_NOPE_DIM:QK_NOPE_DIM + ROPE_HALF]
    q_x2 = wq[:, :, QK_NOPE_DIM + ROPE_HALF:]
    wqn = wq[:, :, :QK_NOPE_DIM]
    wqr = wq[:, :, QK_NOPE_DIM:]
    wqb = jnp.concatenate([-q_x2, q_x1], axis=-1)
    k_x1 = w_dkv[:, KV_LORA_RANK:KV_LORA_RANK + ROPE_HALF]
    k_x2 = w_dkv[:, KV_LORA_RANK + ROPE_HALF:]
    kpad = jnp.zeros((D_MODEL, LANES - QK_ROPE_DIM), F32)
    wkv = jnp.concatenate([w_dkv[:, :KV_LORA_RANK], k_x1, k_x2, kpad, -k_x2, k_x1, kpad], axis=-1)
    return dict(
        pre=row(pre_g), wdq=w_dq.astype(BF16), qg=row(q_norm_g),
        wqn=wqn.reshape(Q_LORA_RANK, N_HEADS * QK_NOPE_DIM).T.astype(BF16),
        wqr=wqr.reshape(Q_LORA_RANK, N_HEADS * QK_ROPE_DIM).T.astype(BF16),
        wqb=wqb.reshape(Q_LORA_RANK, N_HEADS * QK_ROPE_DIM).T.astype(BF16),
        wkv=wkv.astype(BF16), kvg=row(kv_norm_g), wuk=w_uk.astype(BF16),
        wuv=w_uv.T.astype(BF16), wo=w_o.astype(BF16), post=row(post_g))


def _rope_tables(seq_len):
    pos = jnp.arange(seq_len, dtype=F32)
    inv_freq = ROPE_THETA ** (-jnp.arange(0, QK_ROPE_DIM, 2, dtype=F32) / QK_ROPE_DIM)
    ang = pos[:, None] * inv_freq[None, :]
    zeros = jnp.zeros((seq_len, LANES - QK_ROPE_DIM), F32)
    cos, sin = jnp.cos(ang), jnp.sin(ang)
    cos_tab = jnp.concatenate([cos, cos, zeros], axis=-1)
    sin_tab = jnp.concatenate([sin, sin, zeros], axis=-1)
    return cos_tab, sin_tab, cos_tab[:, :QK_ROPE_DIM].T, sin_tab[:, :QK_ROPE_DIM].T


def kernel(x_prompt, x_sample, l0_ffn1_pre_g, l0_ffn1_w_gate, l0_ffn1_w_up, l0_ffn1_w_down, l0_ffn1_post_g, l0_mix_pre_g, l0_conv_w_pw1, l0_conv_b_pw1, l0_conv_w_dw, l0_conv_b_dw, l0_conv_ln_g, l0_conv_ln_b, l0_conv_w_pw2, l0_conv_b_pw2, l0_mix_post_g, l0_ffn2_pre_g, l0_ffn2_w_gate, l0_ffn2_w_up, l0_ffn2_w_down, l0_ffn2_post_g, l1_ffn1_pre_g, l1_ffn1_w_gate, l1_ffn1_w_up, l1_ffn1_w_down, l1_ffn1_post_g, l1_mix_pre_g, l1_mla_w_dq, l1_mla_q_norm_g, l1_mla_w_uq, l1_mla_w_dkv, l1_mla_kv_norm_g, l1_mla_w_uk, l1_mla_w_uv, l1_mla_w_o, l1_mix_post_g, l1_ffn2_pre_g, l1_ffn2_w_gate, l1_ffn2_w_up, l1_ffn2_w_down, l1_ffn2_post_g):
    l0_ffn1 = _prep_ffn(l0_ffn1_pre_g, l0_ffn1_w_gate, l0_ffn1_w_up, l0_ffn1_w_down, l0_ffn1_post_g)
    l0_ffn2 = _prep_ffn(l0_ffn2_pre_g, l0_ffn2_w_gate, l0_ffn2_w_up, l0_ffn2_w_down, l0_ffn2_post_g)
    l1_ffn1 = _prep_ffn(l1_ffn1_pre_g, l1_ffn1_w_gate, l1_ffn1_w_up, l1_ffn1_w_down, l1_ffn1_post_g)
    l1_ffn2 = _prep_ffn(l1_ffn2_pre_g, l1_ffn2_w_gate, l1_ffn2_w_up, l1_ffn2_w_down, l1_ffn2_post_g)
    conv = _prep_conv(l0_mix_pre_g, l0_conv_w_pw1, l0_conv_b_pw1, l0_conv_w_dw, l0_conv_b_dw,
                      l0_conv_ln_g, l0_conv_ln_b, l0_conv_w_pw2, l0_conv_b_pw2, l0_mix_post_g)
    mla = _prep_mla(l1_mix_pre_g, l1_mla_w_dq, l1_mla_q_norm_g, l1_mla_w_uq, l1_mla_w_dkv,
                    l1_mla_kv_norm_g, l1_mla_w_uk, l1_mla_w_uv, l1_mla_w_o, l1_mix_post_g)

    def trunk(x):
        shape = x.shape
        flat = lambda a: a.reshape(-1, D_MODEL)
        tabs = _rope_tables(shape[1])
        x = _ffn(flat(x), l0_ffn1).reshape(shape)
        x = _conv(x, conv)
        x = _ffn(flat(x), l0_ffn2)
        x = _ffn(x, l1_ffn1).reshape(shape)
        attn = _mla_attention(x, mla, tabs)
        x = _ffn(flat(x), l1_ffn2, attn=(attn, mla["wo"], mla["post"])).reshape(shape)
        return x

    return (trunk(x_prompt), trunk(x_sample))
```

```python
import functools

import jax
import jax.numpy as jnp
from jax import lax
from jax.experimental import pallas as pl
from jax.experimental.pallas import tpu as pltpu

F32 = jnp.float32
BF16 = jnp.bfloat16

D_MODEL = 1024
D_FF = 2816
FFN_RESIDUAL_WEIGHT = 0.5
CONV_WIDTH = 31
CONV_HALF = CONV_WIDTH // 2
N_HEADS = 8
QK_NOPE_DIM = 128
QK_ROPE_DIM = 64
ROPE_HALF = QK_ROPE_DIM // 2
V_HEAD_DIM = 128
Q_LORA_RANK = 384
KV_LORA_RANK = 256
ROPE_THETA = 10000.0
NORM_EPS = 1e-6

LANES = 128
SUBLANES = 8
MXU_DIM = 256
QK_PAD_DIM = 256
VMEM_LIMIT_BYTES = 56 * 1024 * 1024

FFN_CHUNK = MXU_DIM
FFN_NUM_CHUNKS = D_FF // FFN_CHUNK
FFN_TOKENS = 512
CONV_TOKENS = 512
CONV_HALO = 16
CONV_ROW_BLOCK = 64
CONV_LANE_BLOCK = 256
MLA_TOKENS = 512
ATTN_Q_BLOCK = 512
ATTN_KV_BLOCK = 2048
ATTN_Q_BLOCKS_PER_STEP = 4
LOG2_E = 1.4426950408889634
NORM_BOUND_MARGIN = 1.02
MAX_SAFE_SHIFT = 40.0


def _rms(x, g):
    ms = jnp.mean(x * x, axis=-1, keepdims=True)
    return x * lax.rsqrt(ms + NORM_EPS) * g


def _silu(x):
    return x / (1.0 + jnp.exp(-x))


def _sigmoid(x):
    return 1.0 / (1.0 + jnp.exp(-x))


def _dot(a, b):
    return jnp.dot(a, b, preferred_element_type=F32)


def _const_spec(shape):
    zeros = (0,) * len(shape)
    return pl.BlockSpec(shape, lambda *_: zeros, pipeline_mode=pl.Buffered(1))


def _ffn_body(with_attn_out, *refs):
    if with_attn_out:
        a_ref, wo_ref, mix_post_ref, x_ref, pre_ref, wg_ref, wu_ref, wd_ref, post_ref, o_ref = refs
        x = x_ref[...] + _rms(_dot(a_ref[...], wo_ref[...]), mix_post_ref[...])
    else:
        x_ref, pre_ref, wg_ref, wu_ref, wd_ref, post_ref, o_ref = refs
        x = x_ref[...]
    o_ref[...] = x
    xn = _rms(x, pre_ref[...])
    acc = None
    for c in range(FFN_NUM_CHUNKS):
        cols = slice(c * FFN_CHUNK, (c + 1) * FFN_CHUNK)
        g = _dot(xn, wg_ref[:, cols])
        u = _dot(xn, wu_ref[:, cols])
        h = _silu(g) * u
        part = _dot(h, wd_ref[cols, :])
        acc = part if acc is None else acc + part
    o_ref[...] += _rms(acc, post_ref[...])


def _ffn(x2d, p, attn=None):
    tokens = x2d.shape[0]
    tm = FFN_TOKENS
    row = lambda w: pl.BlockSpec((tm, w), lambda i: (i, 0))
    attn_args, attn_specs = (), []
    if attn is not None:
        attn_args = attn
        attn_specs = [row(attn[0].shape[1]), _const_spec(attn[1].shape), _const_spec((1, D_MODEL))]
    return pl.pallas_call(
        functools.partial(_ffn_body, attn is not None),
        grid=(tokens // tm,),
        in_specs=attn_specs + [
            row(D_MODEL),
            _const_spec((1, D_MODEL)),
            _const_spec((D_MODEL, D_FF)),
            _const_spec((D_MODEL, D_FF)),
            _const_spec((D_FF, D_MODEL)),
            _const_spec((1, D_MODEL)),
        ],
        out_specs=row(D_MODEL),
        out_shape=jax.ShapeDtypeStruct((tokens, D_MODEL), F32),
        compiler_params=pltpu.CompilerParams(
            dimension_semantics=("parallel",), vmem_limit_bytes=VMEM_LIMIT_BYTES),
        name="ffn_attn_out" if attn is not None else "ffn",
    )(*attn_args, x2d, p["pre"], p["wg"], p["wu"], p["wd"], p["post"])


def _prep_ffn(pre_g, w_gate, w_up, w_down, post_g):
    return dict(
        pre=pre_g.reshape(1, D_MODEL), post=(FFN_RESIDUAL_WEIGHT * post_g).reshape(1, D_MODEL),
        wg=w_gate, wu=w_up, wd=w_down)


def _conv_body(seq_len, xm_ref, xp_ref, xq_ref, pre_ref, w1_ref, b1_ref, wdw_ref, bdw_ref,
               lng_ref, lnb_ref, w2_ref, b2_ref, post_ref, o_ref, xn_ref, h_ref, c_ref):
    tm = CONV_TOKENS
    halo = CONV_HALO
    rows = tm + 2 * halo
    i = pl.program_id(1)
    pre = pre_ref[...]
    xn_ref[0:halo, :] = _rms(xp_ref[0], pre).astype(BF16)
    xn_ref[halo:halo + tm, :] = _rms(xm_ref[0], pre).astype(BF16)
    xn_ref[halo + tm:rows, :] = _rms(xq_ref[0], pre).astype(BF16)

    tok = i * tm - halo + lax.broadcasted_iota(jnp.int32, (rows, 1), 0)
    inside = jnp.logical_and(tok >= 0, tok < seq_len)
    lb = CONV_LANE_BLOCK
    for n in range(D_MODEL // lb):
        xn = xn_ref[...]
        val = _dot(xn, w1_ref[:, n * lb:(n + 1) * lb]) + b1_ref[:, n * lb:(n + 1) * lb]
        gate = (_dot(xn, w1_ref[:, D_MODEL + n * lb:D_MODEL + (n + 1) * lb])
                + b1_ref[:, D_MODEL + n * lb:D_MODEL + (n + 1) * lb])
        h = jnp.where(inside, val * _sigmoid(gate), 0.0)
        for j in range(lb // LANES):
            h_ref[n * (lb // LANES) + j] = h[:, j * LANES:(j + 1) * LANES]

    rb = CONV_ROW_BLOCK
    shift = halo - CONV_HALF
    for j in range(D_MODEL // LANES):
        lanes = slice(j * LANES, (j + 1) * LANES)
        for r in range(tm // rb):
            r0 = r * rb
            acc = jnp.broadcast_to(bdw_ref[:, lanes], (rb, LANES))
            for k in range(CONV_WIDTH):
                acc = acc + h_ref[j, r0 + shift + k:r0 + shift + k + rb, :] * wdw_ref[k:k + 1, lanes]
            c_ref[r0:r0 + rb, lanes] = acc

    c = c_ref[...]
    mu = jnp.mean(c, axis=-1, keepdims=True)
    cc = c - mu
    var = jnp.mean(cc * cc, axis=-1, keepdims=True)
    y = cc * lax.rsqrt(var + NORM_EPS) * lng_ref[...] + lnb_ref[...]
    z = _dot(_silu(y).astype(BF16), w2_ref[...]) + b2_ref[...]
    o_ref[0] = xm_ref[0] + _rms(z, post_ref[...])


def _conv(x, p):
    batch, seq_len, _ = x.shape
    tm = CONV_TOKENS
    halo = CONV_HALO
    per_tile = tm // halo
    last_halo_block = seq_len // halo - 1
    main = pl.BlockSpec((1, tm, D_MODEL), lambda b, i: (b, i, 0))
    prev = pl.BlockSpec((1, halo, D_MODEL),
                        lambda b, i: (b, jnp.maximum(i * per_tile - 1, 0), 0))
    nxt = pl.BlockSpec((1, halo, D_MODEL),
                       lambda b, i: (b, jnp.minimum((i + 1) * per_tile, last_halo_block), 0))
    return pl.pallas_call(
        functools.partial(_conv_body, seq_len),
        grid=(batch, seq_len // tm),
        in_specs=[
            main, prev, nxt,
            _const_spec((1, D_MODEL)),
            _const_spec((D_MODEL, 2 * D_MODEL)),
            _const_spec((1, 2 * D_MODEL)),
            _const_spec((CONV_WIDTH, D_MODEL)),
            _const_spec((1, D_MODEL)),
            _const_spec((1, D_MODEL)),
            _const_spec((1, D_MODEL)),
            _const_spec((D_MODEL, D_MODEL)),
            _const_spec((1, D_MODEL)),
            _const_spec((1, D_MODEL)),
        ],
        out_specs=main,
        out_shape=jax.ShapeDtypeStruct(x.shape, F32),
        scratch_shapes=[
            pltpu.VMEM((tm + 2 * halo, D_MODEL), BF16),
            pltpu.VMEM((D_MODEL // LANES, tm + 2 * halo, LANES), F32),
            pltpu.VMEM((tm, D_MODEL), F32),
        ],
        compiler_params=pltpu.CompilerParams(
            dimension_semantics=("parallel", "parallel"), vmem_limit_bytes=VMEM_LIMIT_BYTES),
        name="conv_module",
    )(x, x, x, p["pre"], p["w1"], p["b1"], p["wdw"], p["bdw"], p["lng"], p["lnb"],
      p["w2"], p["b2"], p["post"])


def _prep_conv(pre_g, w_pw1, b_pw1, w_dw, b_dw, ln_g, ln_b, w_pw2, b_pw2, post_g):
    row = lambda v: v.reshape(1, -1)
    return dict(pre=row(pre_g), w1=w_pw1.astype(BF16), b1=row(b_pw1), wdw=w_dw, bdw=row(b_dw),
                lng=row(ln_g), lnb=row(ln_b), w2=w_pw2.astype(BF16), b2=row(b_pw2),
                post=row(post_g))


def _mla_proj_body(x_ref, pre_ref, wdq_ref, qg_ref, wqn_ref, wqr_ref, wqb_ref, wkv_ref, kvg_ref,
                   wuk_ref, wuv_ref, cos_ref, sin_ref, cos_t_ref, sin_t_ref,
                   qt_ref, k_ref, vt_ref, kmax_ref):
    scale = (QK_NOPE_DIM + QK_ROPE_DIM) ** -0.5 * LOG2_E
    nt = (((1,), (1,)), ((), ()))
    xn = _rms(x_ref[0], pre_ref[...]).astype(BF16)
    cq = _rms(_dot(xn, wdq_ref[...]), qg_ref[...]).astype(BF16)
    kv = _dot(xn, wkv_ref[...])
    ckv = _rms(kv[:, :KV_LORA_RANK], kvg_ref[...]).astype(BF16)
    k_rope = (kv[:, KV_LORA_RANK:KV_LORA_RANK + LANES] * cos_ref[...]
              + kv[:, KV_LORA_RANK + LANES:] * sin_ref[...])
    k_rope_sq = jnp.sum(k_rope * k_rope, axis=1, keepdims=True)
    k_rope = k_rope.astype(BF16)
    k_nope = _dot(ckv, wuk_ref[...])
    k_sq_max = None
    for h in range(N_HEADS):
        lo = h * QK_PAD_DIM
        kn = k_nope[:, h * QK_NOPE_DIM:(h + 1) * QK_NOPE_DIM]
        k_sq = jnp.sum(kn * kn, axis=1, keepdims=True) + k_rope_sq
        k_sq_max = k_sq if k_sq_max is None else jnp.maximum(k_sq_max, k_sq)
        k_ref[0, :, lo:lo + LANES] = kn.astype(BF16)
        k_ref[0, :, lo + LANES:lo + QK_PAD_DIM] = k_rope
    kmax_ref[...] = jnp.broadcast_to(jnp.sqrt(jnp.max(k_sq_max, axis=0, keepdims=True)),
                                     kmax_ref.shape)
    vt_ref[0] = lax.dot_general(wuv_ref[...], ckv, nt, preferred_element_type=F32).astype(BF16)
    cos_t = cos_t_ref[...]
    sin_t = sin_t_ref[...]
    q_nope_all = lax.dot_general(wqn_ref[...], cq, nt, preferred_element_type=F32) * scale
    q_rope_all = lax.dot_general(wqr_ref[...], cq, nt, preferred_element_type=F32)
    q_rot_all = lax.dot_general(wqb_ref[...], cq, nt, preferred_element_type=F32)
    last_row = lax.broadcasted_iota(jnp.int32, cos_t.shape, 0) == QK_ROPE_DIM - 1
    for h in range(N_HEADS):
        lo = h * QK_PAD_DIM
        rope_rows = slice(h * QK_ROPE_DIM, (h + 1) * QK_ROPE_DIM)
        q_nope = q_nope_all[h * QK_NOPE_DIM:(h + 1) * QK_NOPE_DIM]
        q_rope = (q_rope_all[rope_rows] * cos_t + q_rot_all[rope_rows] * sin_t) * scale
        q_norm = jnp.sqrt(jnp.sum(q_nope * q_nope, axis=0, keepdims=True)
                          + jnp.sum(q_rope * q_rope, axis=0, keepdims=True))
        qt_ref[0, lo:lo + QK_NOPE_DIM, :] = q_nope.astype(BF16)
        qt_ref[0, lo + QK_NOPE_DIM:lo + QK_NOPE_DIM + QK_ROPE_DIM, :] = q_rope.astype(BF16)
        qt_ref[0, lo + QK_NOPE_DIM + QK_ROPE_DIM:lo + QK_PAD_DIM, :] = jnp.where(
            last_row, q_norm, 0.0).astype(BF16)


def _attn_body(seq_len, qt_ref, k_ref, vt_ref, kmax_ref, o_ref):
    tq = ATTN_Q_BLOCK
    n_kv = seq_len // ATTN_KV_BLOCK
    stages = [(qb, j) for qb in range(ATTN_Q_BLOCKS_PER_STEP) for j in range(n_kv)]

    def scores(stage):
        qb, j = stage
        return _dot(k_ref[0, j * ATTN_KV_BLOCK:(j + 1) * ATTN_KV_BLOCK, :],
                    qt_ref[0, :, qb * tq:(qb + 1) * tq])

    def attend(shift):
        m = l = acc = None
        s_next = scores(stages[0])
        for idx, (qb, j) in enumerate(stages):
            s = s_next
            if idx + 1 < len(stages):
                s_next = scores(stages[idx + 1])
            vt = vt_ref[0, :, j * ATTN_KV_BLOCK:(j + 1) * ATTN_KV_BLOCK]
            if shift is not None:
                p = jnp.exp2(s - shift[:, qb * tq:(qb + 1) * tq])
                p_sum = jnp.sum(p, axis=0, keepdims=True)
                pv = _dot(vt, p.astype(BF16))
                l, acc = (p_sum, pv) if j == 0 else (l + p_sum, acc + pv)
            elif j == 0:
                m = jnp.max(s, axis=0, keepdims=True)
                p = jnp.exp2(s - m)
                l = jnp.sum(p, axis=0, keepdims=True)
                acc = _dot(vt, p.astype(BF16))
            else:
                m_new = jnp.maximum(m, jnp.max(s, axis=0, keepdims=True))
                alpha = jnp.exp2(m - m_new)
                p = jnp.exp2(s - m_new)
                l = alpha * l + jnp.sum(p, axis=0, keepdims=True)
                acc = alpha * acc + _dot(vt, p.astype(BF16))
                m = m_new
            if j == n_kv - 1:
                o_ref[0, qb * tq:(qb + 1) * tq, :] = (acc / l).T.astype(BF16)

    q_norm = qt_ref[0, QK_PAD_DIM - 1:QK_PAD_DIM, :].astype(F32)
    k_norm_max = NORM_BOUND_MARGIN * jnp.max(kmax_ref[...])
    bound = k_norm_max * q_norm
    bound_is_safe = k_norm_max * jnp.max(q_norm) <= MAX_SAFE_SHIFT

    @pl.when(bound_is_safe)
    def _():
        attend(bound)

    @pl.when(jnp.logical_not(bound_is_safe))
    def _():
        attend(None)


def _mla_attention(x, p, tabs):
    batch, seq_len, _ = x.shape
    tm = MLA_TOKENS
    qk_width = N_HEADS * QK_PAD_DIM
    v_width = N_HEADS * V_HEAD_DIM
    tile = lambda w: pl.BlockSpec((1, tm, w), lambda b, i: (b, i, 0))
    tile_t = lambda w: pl.BlockSpec((1, w, tm), lambda b, i: (b, 0, i))
    tab = pl.BlockSpec((tm, LANES), lambda b, i: (i, 0))
    tab_t = pl.BlockSpec((QK_ROPE_DIM, tm), lambda b, i: (0, i))
    qt, k, vt, kmax = pl.pallas_call(
        _mla_proj_body,
        grid=(batch, seq_len // tm),
        in_specs=[
            tile(D_MODEL),
            _const_spec((1, D_MODEL)),
            _const_spec((D_MODEL, Q_LORA_RANK)),
            _const_spec((1, Q_LORA_RANK)),
            _const_spec((N_HEADS * QK_NOPE_DIM, Q_LORA_RANK)),
            _const_spec((N_HEADS * QK_ROPE_DIM, Q_LORA_RANK)),
            _const_spec((N_HEADS * QK_ROPE_DIM, Q_LORA_RANK)),
            _const_spec((D_MODEL, KV_LORA_RANK + 2 * LANES)),
            _const_spec((1, KV_LORA_RANK)),
            _const_spec((KV_LORA_RANK, N_HEADS * QK_NOPE_DIM)),
            _const_spec((v_width, KV_LORA_RANK)),
            tab, tab, tab_t, tab_t,
        ],
        out_specs=[tile_t(qk_width), tile(qk_width), tile_t(v_width),
                   pl.BlockSpec((1, 1, SUBLANES, LANES), lambda b, i: (b, i, 0, 0))],
        out_shape=[
            jax.ShapeDtypeStruct((batch, qk_width, seq_len), BF16),
            jax.ShapeDtypeStruct((batch, seq_len, qk_width), BF16),
            jax.ShapeDtypeStruct((batch, v_width, seq_len), BF16),
            jax.ShapeDtypeStruct((batch, seq_len // tm, SUBLANES, LANES), F32),
        ],
        compiler_params=pltpu.CompilerParams(
            dimension_semantics=("parallel", "parallel"), vmem_limit_bytes=VMEM_LIMIT_BYTES),
        name="mla_proj",
    )(x, p["pre"], p["wdq"], p["qg"], p["wqn"], p["wqr"], p["wqb"], p["wkv"], p["kvg"], p["wuk"],
      p["wuv"], *tabs)

    tq = ATTN_Q_BLOCK * ATTN_Q_BLOCKS_PER_STEP
    attn = pl.pallas_call(
        functools.partial(_attn_body, seq_len),
        grid=(batch, N_HEADS, seq_len // tq),
        in_specs=[
            pl.BlockSpec((1, QK_PAD_DIM, tq), lambda b, h, i: (b, h, i)),
            pl.BlockSpec((1, seq_len, QK_PAD_DIM), lambda b, h, i: (b, 0, h)),
            pl.BlockSpec((1, V_HEAD_DIM, seq_len), lambda b, h, i: (b, h, 0)),
            pl.BlockSpec((1, seq_len // tm, SUBLANES, LANES), lambda b, h, i: (b, 0, 0, 0)),
        ],
        out_specs=pl.BlockSpec((1, tq, V_HEAD_DIM), lambda b, h, i: (b, i, h)),
        out_shape=jax.ShapeDtypeStruct((batch, seq_len, v_width), BF16),
        compiler_params=pltpu.CompilerParams(
            dimension_semantics=("parallel", "parallel", "parallel"),
            vmem_limit_bytes=VMEM_LIMIT_BYTES),
        name="mla_attention",
    )(qt, k, vt, kmax)

    return attn.reshape(batch * seq_len, v_width)


def _prep_mla(pre_g, w_dq, q_norm_g, w_uq, w_dkv, kv_norm_g, w_uk, w_uv, w_o, post_g):
    row = lambda v: v.reshape(1, -1)
    head_dim = QK_NOPE_DIM + QK_ROPE_DIM
    wq = w_uq.reshape(Q_LORA_RANK, N_HEADS, head_dim)
    q_x1 = wq[:, :, QK_NOPE_DIM:QK_NOPE_DIM + ROPE_HALF]
    q_x2 = wq[:, :, QK_NOPE_DIM + ROPE_HALF:]
    wqn = wq[:, :, :QK_NOPE_DIM]
    wqr = wq[:, :, QK_NOPE_DIM:]
    wqb = jnp.concatenate([-q_x2, q_x1], axis=-1)
    k_x1 = w_dkv[:, KV_LORA_RANK:KV_LORA_RANK + ROPE_HALF]
    k_x2 = w_dkv[:, KV_LORA_RANK + ROPE_HALF:]
    kpad = jnp.zeros((D_MODEL, LANES - QK_ROPE_DIM), F32)
    wkv = jnp.concatenate([w_dkv[:, :KV_LORA_RANK], k_x1, k_x2, kpad, -k_x2, k_x1, kpad], axis=-1)
    return dict(
        pre=row(pre_g), wdq=w_dq.astype(BF16), qg=row(q_norm_g),
        wqn=wqn.reshape(Q_LORA_RANK, N_HEADS * QK_NOPE_DIM).T.astype(BF16),
        wqr=wqr.reshape(Q_LORA_RANK, N_HEADS * QK_ROPE_DIM).T.astype(BF16),
        wqb=wqb.reshape(Q_LORA_RANK, N_HEADS * QK_ROPE_DIM).T.astype(BF16),
        wkv=wkv.astype(BF16), kvg=row(kv_norm_g), wuk=w_uk.astype(BF16),
        wuv=w_uv.T.astype(BF16), wo=w_o.astype(BF16), post=row(post_g))


def _rope_tables(seq_len):
    pos = jnp.arange(seq_len, dtype=F32)
    inv_freq = ROPE_THETA ** (-jnp.arange(0, QK_ROPE_DIM, 2, dtype=F32) / QK_ROPE_DIM)
    ang = pos[:, None] * inv_freq[None, :]
    zeros = jnp.zeros((seq_len, LANES - QK_ROPE_DIM), F32)
    cos, sin = jnp.cos(ang), jnp.sin(ang)
    cos_tab = jnp.concatenate([cos, cos, zeros], axis=-1)
    sin_tab = jnp.concatenate([sin, sin, zeros], axis=-1)
    return cos_tab, sin_tab, cos_tab[:, :QK_ROPE_DIM].T, sin_tab[:, :QK_ROPE_DIM].T


def kernel(x_prompt, x_sample, l0_ffn1_pre_g, l0_ffn1_w_gate, l0_ffn1_w_up, l0_ffn1_w_down, l0_ffn1_post_g, l0_mix_pre_g, l0_conv_w_pw1, l0_conv_b_pw1, l0_conv_w_dw, l0_conv_b_dw, l0_conv_ln_g, l0_conv_ln_b, l0_conv_w_pw2, l0_conv_b_pw2, l0_mix_post_g, l0_ffn2_pre_g, l0_ffn2_w_gate, l0_ffn2_w_up, l0_ffn2_w_down, l0_ffn2_post_g, l1_ffn1_pre_g, l1_ffn1_w_gate, l1_ffn1_w_up, l1_ffn1_w_down, l1_ffn1_post_g, l1_mix_pre_g, l1_mla_w_dq, l1_mla_q_norm_g, l1_mla_w_uq, l1_mla_w_dkv, l1_mla_kv_norm_g, l1_mla_w_uk, l1_mla_w_uv, l1_mla_w_o, l1_mix_post_g, l1_ffn2_pre_g, l1_ffn2_w_gate, l1_ffn2_w_up, l1_ffn2_w_down, l1_ffn2_post_g):
    l0_ffn1 = _prep_ffn(l0_ffn1_pre_g, l0_ffn1_w_gate, l0_ffn1_w_up, l0_ffn1_w_down, l0_ffn1_post_g)
    l0_ffn2 = _prep_ffn(l0_ffn2_pre_g, l0_ffn2_w_gate, l0_ffn2_w_up, l0_ffn2_w_down, l0_ffn2_post_g)
    l1_ffn1 = _prep_ffn(l1_ffn1_pre_g, l1_ffn1_w_gate, l1_ffn1_w_up, l1_ffn1_w_down, l1_ffn1_post_g)
    l1_ffn2 = _prep_ffn(l1_ffn2_pre_g, l1_ffn2_w_gate, l1_ffn2_w_up, l1_ffn2_w_down, l1_ffn2_post_g)
    conv = _prep_conv(l0_mix_pre_g, l0_conv_w_pw1, l0_conv_b_pw1, l0_conv_w_dw, l0_conv_b_dw,
                      l0_conv_ln_g, l0_conv_ln_b, l0_conv_w_pw2, l0_conv_b_pw2, l0_mix_post_g)
    mla = _prep_mla(l1_mix_pre_g, l1_mla_w_dq, l1_mla_q_norm_g, l1_mla_w_uq, l1_mla_w_dkv,
                    l1_mla_kv_norm_g, l1_mla_w_uk, l1_mla_w_uv, l1_mla_w_o, l1_mix_post_g)

    def trunk(x):
        shape = x.shape
        flat = lambda a: a.reshape(-1, D_MODEL)
        tabs = _rope_tables(shape[1])
        x = _ffn(flat(x), l0_ffn1).reshape(shape)
        x = _conv(x, conv)
        x = _ffn(flat(x), l0_ffn2)
        x = _ffn(x, l1_ffn1).reshape(shape)
        attn = _mla_attention(x, mla, tabs)
        x = _ffn(flat(x), l1_ffn2, attn=(attn, mla["wo"], mla["post"])).reshape(shape)
        return x

    return (trunk(x_prompt), trunk(x_sample))
```

```python
import functools

import jax
import jax.numpy as jnp
from jax import lax
from jax.experimental import pallas as pl
from jax.experimental.pallas import tpu as pltpu

F32 = jnp.float32
BF16 = jnp.bfloat16

D_MODEL = 1024
D_FF = 2816
FFN_RESIDUAL_WEIGHT = 0.5
CONV_WIDTH = 31
CONV_HALF = CONV_WIDTH // 2
N_HEADS = 8
QK_NOPE_DIM = 128
QK_ROPE_DIM = 64
ROPE_HALF = QK_ROPE_DIM // 2
V_HEAD_DIM = 128
Q_LORA_RANK = 384
KV_LORA_RANK = 256
ROPE_THETA = 10000.0
NORM_EPS = 1e-6

LANES = 128
SUBLANES = 8
MXU_DIM = 256
QK_PAD_DIM = 256
VMEM_LIMIT_BYTES = 56 * 1024 * 1024

FFN_CHUNK = MXU_DIM
FFN_NUM_CHUNKS = D_FF // FFN_CHUNK
FFN_TOKENS = 512
CONV_TOKENS = 512
CONV_HALO = 16
CONV_ROW_BLOCK = 64
CONV_LANE_BLOCK = 256
MLA_TOKENS = 1024
ATTN_Q_BLOCK = 512
ATTN_KV_BLOCK = 2048
ATTN_Q_BLOCKS_PER_STEP = 4
LOG2_E = 1.4426950408889634
NORM_BOUND_MARGIN = 1.02
MAX_SAFE_SHIFT = 40.0


def _rms(x, g):
    ms = jnp.mean(x * x, axis=-1, keepdims=True)
    return x * lax.rsqrt(ms + NORM_EPS) * g


def _silu(x):
    return x / (1.0 + jnp.exp(-x))


def _sigmoid(x):
    return 1.0 / (1.0 + jnp.exp(-x))


def _dot(a, b):
    return jnp.dot(a, b, preferred_element_type=F32)


def _const_spec(shape):
    zeros = (0,) * len(shape)
    return pl.BlockSpec(shape, lambda *_: zeros, pipeline_mode=pl.Buffered(1))


def _ffn_body(with_attn_out, *refs):
    if with_attn_out:
        a_ref, wo_ref, mix_post_ref, x_ref, pre_ref, wg_ref, wu_ref, wd_ref, post_ref, o_ref = refs
        x = x_ref[...] + _rms(_dot(a_ref[...], wo_ref[...]), mix_post_ref[...])
    else:
        x_ref, pre_ref, wg_ref, wu_ref, wd_ref, post_ref, o_ref = refs
        x = x_ref[...]
    o_ref[...] = x
    xn = _rms(x, pre_ref[...])
    acc = None
    for c in range(FFN_NUM_CHUNKS):
        cols = slice(c * FFN_CHUNK, (c + 1) * FFN_CHUNK)
        g = _dot(xn, wg_ref[:, cols])
        u = _dot(xn, wu_ref[:, cols])
        h = _silu(g) * u
        part = _dot(h, wd_ref[cols, :])
        acc = part if acc is None else acc + part
    o_ref[...] += _rms(acc, post_ref[...])


def _ffn(x2d, p, attn=None):
    tokens = x2d.shape[0]
    tm = FFN_TOKENS
    row = lambda w: pl.BlockSpec((tm, w), lambda i: (i, 0))
    attn_args, attn_specs = (), []
    if attn is not None:
        attn_args = attn
        attn_specs = [row(attn[0].shape[1]), _const_spec(attn[1].shape), _const_spec((1, D_MODEL))]
    return pl.pallas_call(
        functools.partial(_ffn_body, attn is not None),
        grid=(tokens // tm,),
        in_specs=attn_specs + [
            row(D_MODEL),
            _const_spec((1, D_MODEL)),
            _const_spec((D_MODEL, D_FF)),
            _const_spec((D_MODEL, D_FF)),
            _const_spec((D_FF, D_MODEL)),
            _const_spec((1, D_MODEL)),
        ],
        out_specs=row(D_MODEL),
        out_shape=jax.ShapeDtypeStruct((tokens, D_MODEL), F32),
        compiler_params=pltpu.CompilerParams(
            dimension_semantics=("parallel",), vmem_limit_bytes=VMEM_LIMIT_BYTES),
        name="ffn_attn_out" if attn is not None else "ffn",
    )(*attn_args, x2d, p["pre"], p["wg"], p["wu"], p["wd"], p["post"])


def _prep_ffn(pre_g, w_gate, w_up, w_down, post_g):
    return dict(
        pre=pre_g.reshape(1, D_MODEL), post=(FFN_RESIDUAL_WEIGHT * post_g).reshape(1, D_MODEL),
        wg=w_gate, wu=w_up, wd=w_down)


def _conv_body(seq_len, xm_ref, xp_ref, xq_ref, pre_ref, w1_ref, b1_ref, wdw_ref, bdw_ref,
               lng_ref, lnb_ref, w2_ref, b2_ref, post_ref, o_ref, xn_ref, h_ref, c_ref):
    tm = CONV_TOKENS
    halo = CONV_HALO
    rows = tm + 2 * halo
    i = pl.program_id(1)
    pre = pre_ref[...]
    xn_ref[0:halo, :] = _rms(xp_ref[0], pre).astype(BF16)
    xn_ref[halo:halo + tm, :] = _rms(xm_ref[0], pre).astype(BF16)
    xn_ref[halo + tm:rows, :] = _rms(xq_ref[0], pre).astype(BF16)

    tok = i * tm - halo + lax.broadcasted_iota(jnp.int32, (rows, 1), 0)
    inside = jnp.logical_and(tok >= 0, tok < seq_len)
    lb = CONV_LANE_BLOCK
    for n in range(D_MODEL // lb):
        xn = xn_ref[...]
        val = _dot(xn, w1_ref[:, n * lb:(n + 1) * lb]) + b1_ref[:, n * lb:(n + 1) * lb]
        gate = (_dot(xn, w1_ref[:, D_MODEL + n * lb:D_MODEL + (n + 1) * lb])
                + b1_ref[:, D_MODEL + n * lb:D_MODEL + (n + 1) * lb])
        h = jnp.where(inside, val * _sigmoid(gate), 0.0)
        for j in range(lb // LANES):
            h_ref[n * (lb // LANES) + j] = h[:, j * LANES:(j + 1) * LANES]

    rb = CONV_ROW_BLOCK
    shift = halo - CONV_HALF
    for j in range(D_MODEL // LANES):
        lanes = slice(j * LANES, (j + 1) * LANES)
        for r in range(tm // rb):
            r0 = r * rb
            acc = jnp.broadcast_to(bdw_ref[:, lanes], (rb, LANES))
            for k in range(CONV_WIDTH):
                acc = acc + h_ref[j, r0 + shift + k:r0 + shift + k + rb, :] * wdw_ref[k:k + 1, lanes]
            c_ref[r0:r0 + rb, lanes] = acc

    c = c_ref[...]
    mu = jnp.mean(c, axis=-1, keepdims=True)
    cc = c - mu
    var = jnp.mean(cc * cc, axis=-1, keepdims=True)
    y = cc * lax.rsqrt(var + NORM_EPS) * lng_ref[...] + lnb_ref[...]
    z = _dot(_silu(y).astype(BF16), w2_ref[...]) + b2_ref[...]
    o_ref[0] = xm_ref[0] + _rms(z, post_ref[...])


def _conv(x, p):
    batch, seq_len, _ = x.shape
    tm = CONV_TOKENS
    halo = CONV_HALO
    per_tile = tm // halo
    last_halo_block = seq_len // halo - 1
    main = pl.BlockSpec((1, tm, D_MODEL), lambda b, i: (b, i, 0))
    prev = pl.BlockSpec((1, halo, D_MODEL),
                        lambda b, i: (b, jnp.maximum(i * per_tile - 1, 0), 0))
    nxt = pl.BlockSpec((1, halo, D_MODEL),
                       lambda b, i: (b, jnp.minimum((i + 1) * per_tile, last_halo_block), 0))
    return pl.pallas_call(
        functools.partial(_conv_body, seq_len),
        grid=(batch, seq_len // tm),
        in_specs=[
            main, prev, nxt,
            _const_spec((1, D_MODEL)),
            _const_spec((D_MODEL, 2 * D_MODEL)),
            _const_spec((1, 2 * D_MODEL)),
            _const_spec((CONV_WIDTH, D_MODEL)),
            _const_spec((1, D_MODEL)),
            _const_spec((1, D_MODEL)),
            _const_spec((1, D_MODEL)),
            _const_spec((D_MODEL, D_MODEL)),
            _const_spec((1, D_MODEL)),
            _const_spec((1, D_MODEL)),
        ],
        out_specs=main,
        out_shape=jax.ShapeDtypeStruct(x.shape, F32),
        scratch_shapes=[
            pltpu.VMEM((tm + 2 * halo, D_MODEL), BF16),
            pltpu.VMEM((D_MODEL // LANES, tm + 2 * halo, LANES), F32),
            pltpu.VMEM((tm, D_MODEL), F32),
        ],
        compiler_params=pltpu.CompilerParams(
            dimension_semantics=("parallel", "parallel"), vmem_limit_bytes=VMEM_LIMIT_BYTES),
        name="conv_module",
    )(x, x, x, p["pre"], p["w1"], p["b1"], p["wdw"], p["bdw"], p["lng"], p["lnb"],
      p["w2"], p["b2"], p["post"])


def _prep_conv(pre_g, w_pw1, b_pw1, w_dw, b_dw, ln_g, ln_b, w_pw2, b_pw2, post_g):
    row = lambda v: v.reshape(1, -1)
    return dict(pre=row(pre_g), w1=w_pw1.astype(BF16), b1=row(b_pw1), wdw=w_dw, bdw=row(b_dw),
                lng=row(ln_g), lnb=row(ln_b), w2=w_pw2.astype(BF16), b2=row(b_pw2),
                post=row(post_g))


def _mla_proj_body(x_ref, pre_ref, wdq_ref, qg_ref, wqn_ref, wqr_ref, wqb_ref, wkv_ref, kvg_ref,
                   wuk_ref, wuv_ref, cos_ref, sin_ref, cos_t_ref, sin_t_ref,
                   qt_ref, k_ref, vt_ref, kmax_ref):
    scale = (QK_NOPE_DIM + QK_ROPE_DIM) ** -0.5 * LOG2_E
    nt = (((1,), (1,)), ((), ()))
    xn = _rms(x_ref[0], pre_ref[...]).astype(BF16)
    cq = _rms(_dot(xn, wdq_ref[...]), qg_ref[...]).astype(BF16)
    kv = _dot(xn, wkv_ref[...])
    ckv = _rms(kv[:, :KV_LORA_RANK], kvg_ref[...]).astype(BF16)
    k_rope = (kv[:, KV_LORA_RANK:KV_LORA_RANK + LANES] * cos_ref[...]
              + kv[:, KV_LORA_RANK + LANES:] * sin_ref[...])
    k_rope_sq = jnp.sum(k_rope * k_rope, axis=1, keepdims=True)
    k_rope = k_rope.astype(BF16)
    k_nope = _dot(ckv, wuk_ref[...])
    k_sq_max = None
    for h in range(N_HEADS):
        lo = h * QK_PAD_DIM
        kn = k_nope[:, h * QK_NOPE_DIM:(h + 1) * QK_NOPE_DIM]
        k_sq = jnp.sum(kn * kn, axis=1, keepdims=True) + k_rope_sq
        k_sq_max = k_sq if k_sq_max is None else jnp.maximum(k_sq_max, k_sq)
        k_ref[0, :, lo:lo + LANES] = kn.astype(BF16)
        k_ref[0, :, lo + LANES:lo + QK_PAD_DIM] = k_rope
    kmax_ref[...] = jnp.broadcast_to(jnp.sqrt(jnp.max(k_sq_max, axis=0, keepdims=True)),
                                     kmax_ref.shape)
    vt_ref[0] = lax.dot_general(wuv_ref[...], ckv, nt, preferred_element_type=F32).astype(BF16)
    cos_t = cos_t_ref[...]
    sin_t = sin_t_ref[...]
    q_nope_all = lax.dot_general(wqn_ref[...], cq, nt, preferred_element_type=F32) * scale
    q_rope_all = lax.dot_general(wqr_ref[...], cq, nt, preferred_element_type=F32)
    q_rot_all = lax.dot_general(wqb_ref[...], cq, nt, preferred_element_type=F32)
    last_row = lax.broadcasted_iota(jnp.int32, cos_t.shape, 0) == QK_ROPE_DIM - 1
    for h in range(N_HEADS):
        lo = h * QK_PAD_DIM
        rope_rows = slice(h * QK_ROPE_DIM, (h + 1) * QK_ROPE_DIM)
        q_nope = q_nope_all[h * QK_NOPE_DIM:(h + 1) * QK_NOPE_DIM]
        q_rope = (q_rope_all[rope_rows] * cos_t + q_rot_all[rope_rows] * sin_t) * scale
        q_norm = jnp.sqrt(jnp.sum(q_nope * q_nope, axis=0, keepdims=True)
                          + jnp.sum(q_rope * q_rope, axis=0, keepdims=True))
        qt_ref[0, lo:lo + QK_NOPE_DIM, :] = q_nope.astype(BF16)
        qt_ref[0, lo + QK_NOPE_DIM:lo + QK_NOPE_DIM + QK_ROPE_DIM, :] = q_rope.astype(BF16)
        qt_ref[0, lo + QK_NOPE_DIM + QK_ROPE_DIM:lo + QK_PAD_DIM, :] = jnp.where(
            last_row, q_norm, 0.0).astype(BF16)


def _attn_body(seq_len, qt_ref, k_ref, vt_ref, kmax_ref, o_ref):
    tq = ATTN_Q_BLOCK
    n_kv = seq_len // ATTN_KV_BLOCK
    stages = [(qb, j) for qb in range(ATTN_Q_BLOCKS_PER_STEP) for j in range(n_kv)]

    def scores(stage):
        qb, j = stage
        return _dot(k_ref[0, j * ATTN_KV_BLOCK:(j + 1) * ATTN_KV_BLOCK, :],
                    qt_ref[0, :, qb * tq:(qb + 1) * tq])

    def attend(shift):
        m = l = acc = None
        s_next = scores(stages[0])
        for idx, (qb, j) in enumerate(stages):
            s = s_next
            if idx + 1 < len(stages):
                s_next = scores(stages[idx + 1])
            vt = vt_ref[0, :, j * ATTN_KV_BLOCK:(j + 1) * ATTN_KV_BLOCK]
            if shift is not None:
                p = jnp.exp2(s - shift[:, qb * tq:(qb + 1) * tq])
                p_sum = jnp.sum(p, axis=0, keepdims=True)
                pv = _dot(vt, p.astype(BF16))
                l, acc = (p_sum, pv) if j == 0 else (l + p_sum, acc + pv)
            elif j == 0:
                m = jnp.max(s, axis=0, keepdims=True)
                p = jnp.exp2(s - m)
                l = jnp.sum(p, axis=0, keepdims=True)
                acc = _dot(vt, p.astype(BF16))
            else:
                m_new = jnp.maximum(m, jnp.max(s, axis=0, keepdims=True))
                alpha = jnp.exp2(m - m_new)
                p = jnp.exp2(s - m_new)
                l = alpha * l + jnp.sum(p, axis=0, keepdims=True)
                acc = alpha * acc + _dot(vt, p.astype(BF16))
                m = m_new
            if j == n_kv - 1:
                o_ref[0, qb * tq:(qb + 1) * tq, :] = (acc / l).T.astype(BF16)

    q_norm = qt_ref[0, QK_PAD_DIM - 1:QK_PAD_DIM, :].astype(F32)
    k_norm_max = NORM_BOUND_MARGIN * jnp.max(kmax_ref[...])
    bound = k_norm_max * q_norm
    bound_is_safe = k_norm_max * jnp.max(q_norm) <= MAX_SAFE_SHIFT

    @pl.when(bound_is_safe)
    def _():
        attend(bound)

    @pl.when(jnp.logical_not(bound_is_safe))
    def _():
        attend(None)


def _mla_attention(x, p, tabs):
    batch, seq_len, _ = x.shape
    tm = MLA_TOKENS
    qk_width = N_HEADS * QK_PAD_DIM
    v_width = N_HEADS * V_HEAD_DIM
    tile = lambda w: pl.BlockSpec((1, tm, w), lambda b, i: (b, i, 0))
    tile_t = lambda w: pl.BlockSpec((1, w, tm), lambda b, i: (b, 0, i))
    tab = pl.BlockSpec((tm, LANES), lambda b, i: (i, 0))
    tab_t = pl.BlockSpec((QK_ROPE_DIM, tm), lambda b, i: (0, i))
    qt, k, vt, kmax = pl.pallas_call(
        _mla_proj_body,
        grid=(batch, seq_len // tm),
        in_specs=[
            tile(D_MODEL),
            _const_spec((1, D_MODEL)),
            _const_spec((D_MODEL, Q_LORA_RANK)),
            _const_spec((1, Q_LORA_RANK)),
            _const_spec((N_HEADS * QK_NOPE_DIM, Q_LORA_RANK)),
            _const_spec((N_HEADS * QK_ROPE_DIM, Q_LORA_RANK)),
            _const_spec((N_HEADS * QK_ROPE_DIM, Q_LORA_RANK)),
            _const_spec((D_MODEL, KV_LORA_RANK + 2 * LANES)),
            _const_spec((1, KV_LORA_RANK)),
            _const_spec((KV_LORA_RANK, N_HEADS * QK_NOPE_DIM)),
            _const_spec((v_width, KV_LORA_RANK)),
            tab, tab, tab_t, tab_t,
        ],
        out_specs=[tile_t(qk_width), tile(qk_width), tile_t(v_width),
                   pl.BlockSpec((1, 1, SUBLANES, LANES), lambda b, i: (b, i, 0, 0))],
        out_shape=[
            jax.ShapeDtypeStruct((batch, qk_width, seq_len), BF16),
            jax.ShapeDtypeStruct((batch, seq_len, qk_width), BF16),
            jax.ShapeDtypeStruct((batch, v_width, seq_len), BF16),
            jax.ShapeDtypeStruct((batch, seq_len // tm, SUBLANES, LANES), F32),
        ],
        compiler_params=pltpu.CompilerParams(
            dimension_semantics=("parallel", "parallel"), vmem_limit_bytes=VMEM_LIMIT_BYTES),
        name="mla_proj",
    )(x, p["pre"], p["wdq"], p["qg"], p["wqn"], p["wqr"], p["wqb"], p["wkv"], p["kvg"], p["wuk"],
      p["wuv"], *tabs)

    tq = ATTN_Q_BLOCK * ATTN_Q_BLOCKS_PER_STEP
    attn = pl.pallas_call(
        functools.partial(_attn_body, seq_len),
        grid=(batch, N_HEADS, seq_len // tq),
        in_specs=[
            pl.BlockSpec((1, QK_PAD_DIM, tq), lambda b, h, i: (b, h, i)),
            pl.BlockSpec((1, seq_len, QK_PAD_DIM), lambda b, h, i: (b, 0, h)),
            pl.BlockSpec((1, V_HEAD_DIM, seq_len), lambda b, h, i: (b, h, 0)),
            pl.BlockSpec((1, seq_len // tm, SUBLANES, LANES), lambda b, h, i: (b, 0, 0, 0)),
        ],
        out_specs=pl.BlockSpec((1, tq, V_HEAD_DIM), lambda b, h, i: (b, i, h)),
        out_shape=jax.ShapeDtypeStruct((batch, seq_len, v_width), BF16),
        compiler_params=pltpu.CompilerParams(
            dimension_semantics=("parallel", "parallel", "parallel"),
            vmem_limit_bytes=VMEM_LIMIT_BYTES),
        name="mla_attention",
    )(qt, k, vt, kmax)

    return attn.reshape(batch * seq_len, v_width)


def _prep_mla(pre_g, w_dq, q_norm_g, w_uq, w_dkv, kv_norm_g, w_uk, w_uv, w_o, post_g):
    row = lambda v: v.reshape(1, -1)
    head_dim = QK_NOPE_DIM + QK_ROPE_DIM
    wq = w_uq.reshape(Q_LORA_RANK, N_HEADS, head_dim)
    q_x1 = wq[:, :, QK_NOPE_DIM:QK_NOPE_DIM + ROPE_HALF]
    q_x2 = wq[:, :, QK_NOPE_DIM + ROPE_HALF:]
    wqn = wq[:, :, :QK_NOPE_DIM]
    wqr = wq[:, :, QK_NOPE_DIM:]
    wqb = jnp.concatenate([-q_x2, q_x1], axis=-1)
    k_x1 = w_dkv[:, KV_LORA_RANK:KV_LORA_RANK + ROPE_HALF]
    k_x2 = w_dkv[:, KV_LORA_RANK + ROPE_HALF:]
    kpad = jnp.zeros((D_MODEL, LANES - QK_ROPE_DIM), F32)
    wkv = jnp.concatenate([w_dkv[:, :KV_LORA_RANK], k_x1, k_x2, kpad, -k_x2, k_x1, kpad], axis=-1)
    return dict(
        pre=row(pre_g), wdq=w_dq.astype(BF16), qg=row(q_norm_g),
        wqn=wqn.reshape(Q_LORA_RANK, N_HEADS * QK_NOPE_DIM).T.astype(BF16),
        wqr=wqr.reshape(Q_LORA_RANK, N_HEADS * QK_ROPE_DIM).T.astype(BF16),
        wqb=wqb.reshape(Q_LORA_RANK, N_HEADS * QK_ROPE_DIM).T.astype(BF16),
        wkv=wkv.astype(BF16), kvg=row(kv_norm_g), wuk=w_uk.astype(BF16),
        wuv=w_uv.T.astype(BF16), wo=w_o.astype(BF16), post=row(post_g))


def _rope_tables(seq_len):
    pos = jnp.arange(seq_len, dtype=F32)
    inv_freq = ROPE_THETA ** (-jnp.arange(0, QK_ROPE_DIM, 2, dtype=F32) / QK_ROPE_DIM)
    ang = pos[:, None] * inv_freq[None, :]
    zeros = jnp.zeros((seq_len, LANES - QK_ROPE_DIM), F32)
    cos, sin = jnp.cos(ang), jnp.sin(ang)
    cos_tab = jnp.concatenate([cos, cos, zeros], axis=-1)
    sin_tab = jnp.concatenate([sin, sin, zeros], axis=-1)
    return cos_tab, sin_tab, cos_tab[:, :QK_ROPE_DIM].T, sin_tab[:, :QK_ROPE_DIM].T


def kernel(x_prompt, x_sample, l0_ffn1_pre_g, l0_ffn1_w_gate, l0_ffn1_w_up, l0_ffn1_w_down, l0_ffn1_post_g, l0_mix_pre_g, l0_conv_w_pw1, l0_conv_b_pw1, l0_conv_w_dw, l0_conv_b_dw, l0_conv_ln_g, l0_conv_ln_b, l0_conv_w_pw2, l0_conv_b_pw2, l0_mix_post_g, l0_ffn2_pre_g, l0_ffn2_w_gate, l0_ffn2_w_up, l0_ffn2_w_down, l0_ffn2_post_g, l1_ffn1_pre_g, l1_ffn1_w_gate, l1_ffn1_w_up, l1_ffn1_w_down, l1_ffn1_post_g, l1_mix_pre_g, l1_mla_w_dq, l1_mla_q_norm_g, l1_mla_w_uq, l1_mla_w_dkv, l1_mla_kv_norm_g, l1_mla_w_uk, l1_mla_w_uv, l1_mla_w_o, l1_mix_post_g, l1_ffn2_pre_g, l1_ffn2_w_gate, l1_ffn2_w_up, l1_ffn2_w_down, l1_ffn2_post_g):
    l0_ffn1 = _prep_ffn(l0_ffn1_pre_g, l0_ffn1_w_gate, l0_ffn1_w_up, l0_ffn1_w_down, l0_ffn1_post_g)
    l0_ffn2 = _prep_ffn(l0_ffn2_pre_g, l0_ffn2_w_gate, l0_ffn2_w_up, l0_ffn2_w_down, l0_ffn2_post_g)
    l1_ffn1 = _prep_ffn(l1_ffn1_pre_g, l1_ffn1_w_gate, l1_ffn1_w_up, l1_ffn1_w_down, l1_ffn1_post_g)
    l1_ffn2 = _prep_ffn(l1_ffn2_pre_g, l1_ffn2_w_gate, l1_ffn2_w_up, l1_ffn2_w_down, l1_ffn2_post_g)
    conv = _prep_conv(l0_mix_pre_g, l0_conv_w_pw1, l0_conv_b_pw1, l0_conv_w_dw, l0_conv_b_dw,
                      l0_conv_ln_g, l0_conv_ln_b, l0_conv_w_pw2, l0_conv_b_pw2, l0_mix_post_g)
    mla = _prep_mla(l1_mix_pre_g, l1_mla_w_dq, l1_mla_q_norm_g, l1_mla_w_uq, l1_mla_w_dkv,
                    l1_mla_kv_norm_g, l1_mla_w_uk, l1_mla_w_uv, l1_mla_w_o, l1_mix_post_g)

    def trunk(x):
        shape = x.shape
        flat = lambda a: a.reshape(-1, D_MODEL)
        tabs = _rope_tables(shape[1])
        x = _ffn(flat(x), l0_ffn1).reshape(shape)
        x = _conv(x, conv)
        x = _ffn(flat(x), l0_ffn2)
        x = _ffn(x, l1_ffn1).reshape(shape)
        attn = _mla_attention(x, mla, tabs)
        x = _ffn(flat(x), l1_ffn2, attn=(attn, mla["wo"], mla["post"])).reshape(shape)
        return x

    return (trunk(x_prompt), trunk(x_sample))
```

```python
import functools

import jax
import jax.numpy as jnp
from jax import lax
from jax.experimental import pallas as pl
from jax.experimental.pallas import tpu as pltpu

F32 = jnp.float32
BF16 = jnp.bfloat16

D_MODEL = 1024
D_FF = 2816
FFN_RESIDUAL_WEIGHT = 0.5
CONV_WIDTH = 31
CONV_HALF = CONV_WIDTH // 2
N_HEADS = 8
QK_NOPE_DIM = 128
QK_ROPE_DIM = 64
ROPE_HALF = QK_ROPE_DIM // 2
V_HEAD_DIM = 128
Q_LORA_RANK = 384
KV_LORA_RANK = 256
ROPE_THETA = 10000.0
NORM_EPS = 1e-6

LANES = 128
SUBLANES = 8
MXU_DIM = 256
QK_PAD_DIM = 256
VMEM_LIMIT_BYTES = 56 * 1024 * 1024

FFN_CHUNK = MXU_DIM
FFN_NUM_CHUNKS = D_FF // FFN_CHUNK
FFN_TOKENS = 512
CONV_TOKENS = 512
CONV_HALO = 16
CONV_ROW_BLOCK = 64
CONV_LANE_BLOCK = 256
MLA_TOKENS = 1024
ATTN_Q_BLOCK = 512
ATTN_KV_BLOCK = 2048
ATTN_Q_BLOCKS_PER_STEP = 4
LOG2_E = 1.4426950408889634
NORM_BOUND_MARGIN = 1.02
MAX_SAFE_SHIFT = 40.0


def _rms(x, g):
    ms = jnp.mean(x * x, axis=-1, keepdims=True)
    return x * lax.rsqrt(ms + NORM_EPS) * g


def _silu(x):
    return x / (1.0 + jnp.exp(-x))


def _sigmoid(x):
    return 1.0 / (1.0 + jnp.exp(-x))


def _dot(a, b):
    return jnp.dot(a, b, preferred_element_type=F32)


def _const_spec(shape):
    zeros = (0,) * len(shape)
    return pl.BlockSpec(shape, lambda *_: zeros, pipeline_mode=pl.Buffered(1))


def _ffn_body(with_attn_out, *refs):
    if with_attn_out:
        a_ref, wo_ref, mix_post_ref, x_ref, pre_ref, wg_ref, wu_ref, wd_ref, post_ref, o_ref = refs
        x = x_ref[...] + _rms(_dot(a_ref[...], wo_ref[...]), mix_post_ref[...])
    else:
        x_ref, pre_ref, wg_ref, wu_ref, wd_ref, post_ref, o_ref = refs
        x = x_ref[...]
    o_ref[...] = x
    xn = _rms(x, pre_ref[...])
    acc = None
    for c in range(FFN_NUM_CHUNKS):
        cols = slice(c * FFN_CHUNK, (c + 1) * FFN_CHUNK)
        g = _dot(xn, wg_ref[:, cols])
        u = _dot(xn, wu_ref[:, cols])
        h = _silu(g) * u
        part = _dot(h, wd_ref[cols, :])
        acc = part if acc is None else acc + part
    o_ref[...] += _rms(acc, post_ref[...])


def _ffn(x2d, p, attn=None):
    tokens = x2d.shape[0]
    tm = FFN_TOKENS
    row = lambda w: pl.BlockSpec((tm, w), lambda i: (i, 0))
    attn_args, attn_specs = (), []
    if attn is not None:
        attn_args = attn
        attn_specs = [row(attn[0].shape[1]), _const_spec(attn[1].shape), _const_spec((1, D_MODEL))]
    return pl.pallas_call(
        functools.partial(_ffn_body, attn is not None),
        grid=(tokens // tm,),
        in_specs=attn_specs + [
            row(D_MODEL),
            _const_spec((1, D_MODEL)),
            _const_spec((D_MODEL, D_FF)),
            _const_spec((D_MODEL, D_FF)),
            _const_spec((D_FF, D_MODEL)),
            _const_spec((1, D_MODEL)),
        ],
        out_specs=row(D_MODEL),
        out_shape=jax.ShapeDtypeStruct((tokens, D_MODEL), F32),
        compiler_params=pltpu.CompilerParams(
            dimension_semantics=("parallel",), vmem_limit_bytes=VMEM_LIMIT_BYTES),
        name="ffn_attn_out" if attn is not None else "ffn",
    )(*attn_args, x2d, p["pre"], p["wg"], p["wu"], p["wd"], p["post"])


def _prep_ffn(pre_g, w_gate, w_up, w_down, post_g):
    return dict(
        pre=pre_g.reshape(1, D_MODEL), post=(FFN_RESIDUAL_WEIGHT * post_g).reshape(1, D_MODEL),
        wg=w_gate, wu=w_up, wd=w_down)


def _conv_body(seq_len, xm_ref, xp_ref, xq_ref, pre_ref, w1_ref, b1_ref, wdw_ref, bdw_ref,
               lng_ref, lnb_ref, w2_ref, b2_ref, post_ref, o_ref, xn_ref, h_ref, c_ref):
    tm = CONV_TOKENS
    halo = CONV_HALO
    rows = tm + 2 * halo
    i = pl.program_id(1)
    pre = pre_ref[...]
    xn_ref[0:halo, :] = _rms(xp_ref[0], pre).astype(BF16)
    xn_ref[halo:halo + tm, :] = _rms(xm_ref[0], pre).astype(BF16)
    xn_ref[halo + tm:rows, :] = _rms(xq_ref[0], pre).astype(BF16)

    tok = i * tm - halo + lax.broadcasted_iota(jnp.int32, (rows, 1), 0)
    inside = jnp.logical_and(tok >= 0, tok < seq_len)
    lb = CONV_LANE_BLOCK
    for n in range(D_MODEL // lb):
        xn = xn_ref[...]
        val = _dot(xn, w1_ref[:, n * lb:(n + 1) * lb]) + b1_ref[:, n * lb:(n + 1) * lb]
        gate = (_dot(xn, w1_ref[:, D_MODEL + n * lb:D_MODEL + (n + 1) * lb])
                + b1_ref[:, D_MODEL + n * lb:D_MODEL + (n + 1) * lb])
        h = jnp.where(inside, val * _sigmoid(gate), 0.0)
        for j in range(lb // LANES):
            h_ref[n * (lb // LANES) + j] = h[:, j * LANES:(j + 1) * LANES]

    rb = CONV_ROW_BLOCK
    shift = halo - CONV_HALF
    for j in range(D_MODEL // LANES):
        lanes = slice(j * LANES, (j + 1) * LANES)
        for r in range(tm // rb):
            r0 = r * rb
            acc = jnp.broadcast_to(bdw_ref[:, lanes], (rb, LANES))
            for k in range(CONV_WIDTH):
                acc = acc + h_ref[j, r0 + shift + k:r0 + shift + k + rb, :] * wdw_ref[k:k + 1, lanes]
            c_ref[r0:r0 + rb, lanes] = acc

    c = c_ref[...]
    mu = jnp.mean(c, axis=-1, keepdims=True)
    cc = c - mu
    var = jnp.mean(cc * cc, axis=-1, keepdims=True)
    y = cc * lax.rsqrt(var + NORM_EPS) * lng_ref[...] + lnb_ref[...]
    hy = 0.5 * y
    z = _dot((hy + hy * jnp.tanh(hy)).astype(BF16), w2_ref[...]) + b2_ref[...]
    o_ref[0] = xm_ref[0] + _rms(z, post_ref[...])


def _conv(x, p):
    batch, seq_len, _ = x.shape
    tm = CONV_TOKENS
    halo = CONV_HALO
    per_tile = tm // halo
    last_halo_block = seq_len // halo - 1
    main = pl.BlockSpec((1, tm, D_MODEL), lambda b, i: (b, i, 0))
    prev = pl.BlockSpec((1, halo, D_MODEL),
                        lambda b, i: (b, jnp.maximum(i * per_tile - 1, 0), 0))
    nxt = pl.BlockSpec((1, halo, D_MODEL),
                       lambda b, i: (b, jnp.minimum((i + 1) * per_tile, last_halo_block), 0))
    return pl.pallas_call(
        functools.partial(_conv_body, seq_len),
        grid=(batch, seq_len // tm),
        in_specs=[
            main, prev, nxt,
            _const_spec((1, D_MODEL)),
            _const_spec((D_MODEL, 2 * D_MODEL)),
            _const_spec((1, 2 * D_MODEL)),
            _const_spec((CONV_WIDTH, D_MODEL)),
            _const_spec((1, D_MODEL)),
            _const_spec((1, D_MODEL)),
            _const_spec((1, D_MODEL)),
            _const_spec((D_MODEL, D_MODEL)),
            _const_spec((1, D_MODEL)),
            _const_spec((1, D_MODEL)),
        ],
        out_specs=main,
        out_shape=jax.ShapeDtypeStruct(x.shape, F32),
        scratch_shapes=[
            pltpu.VMEM((tm + 2 * halo, D_MODEL), BF16),
            pltpu.VMEM((D_MODEL // LANES, tm + 2 * halo, LANES), F32),
            pltpu.VMEM((tm, D_MODEL), F32),
        ],
        compiler_params=pltpu.CompilerParams(
            dimension_semantics=("parallel", "parallel"), vmem_limit_bytes=VMEM_LIMIT_BYTES),
        name="conv_module",
    )(x, x, x, p["pre"], p["w1"], p["b1"], p["wdw"], p["bdw"], p["lng"], p["lnb"],
      p["w2"], p["b2"], p["post"])


def _prep_conv(pre_g, w_pw1, b_pw1, w_dw, b_dw, ln_g, ln_b, w_pw2, b_pw2, post_g):
    row = lambda v: v.reshape(1, -1)
    return dict(pre=row(pre_g), w1=w_pw1.astype(BF16), b1=row(b_pw1), wdw=w_dw, bdw=row(b_dw),
                lng=row(ln_g), lnb=row(ln_b), w2=w_pw2.astype(BF16), b2=row(b_pw2),
                post=row(post_g))


def _mla_proj_body(x_ref, pre_ref, wdq_ref, qg_ref, wqn_ref, wqr_ref, wqb_ref, wkv_ref, kvg_ref,
                   wuk_ref, wuv_ref, cos_ref, sin_ref, cos_t_ref, sin_t_ref,
                   qt_ref, k_ref, vt_ref, kmax_ref):
    scale = (QK_NOPE_DIM + QK_ROPE_DIM) ** -0.5 * LOG2_E
    nt = (((1,), (1,)), ((), ()))
    xn = _rms(x_ref[0], pre_ref[...]).astype(BF16)
    cq = _rms(_dot(xn, wdq_ref[...]), qg_ref[...]).astype(BF16)
    kv = _dot(xn, wkv_ref[...])
    ckv = _rms(kv[:, :KV_LORA_RANK], kvg_ref[...]).astype(BF16)
    k_rope = (kv[:, KV_LORA_RANK:KV_LORA_RANK + LANES] * cos_ref[...]
              + kv[:, KV_LORA_RANK + LANES:] * sin_ref[...])
    k_rope_sq = jnp.sum(k_rope * k_rope, axis=1, keepdims=True)
    k_rope = k_rope.astype(BF16)
    k_nope = _dot(ckv, wuk_ref[...])
    k_sq_max = None
    for h in range(N_HEADS):
        lo = h * QK_PAD_DIM
        kn = k_nope[:, h * QK_NOPE_DIM:(h + 1) * QK_NOPE_DIM]
        k_sq = jnp.sum(kn * kn, axis=1, keepdims=True) + k_rope_sq
        k_sq_max = k_sq if k_sq_max is None else jnp.maximum(k_sq_max, k_sq)
        k_ref[0, :, lo:lo + LANES] = kn.astype(BF16)
        k_ref[0, :, lo + LANES:lo + QK_PAD_DIM] = k_rope
    kmax_ref[...] = jnp.broadcast_to(jnp.sqrt(jnp.max(k_sq_max, axis=0, keepdims=True)),
                                     kmax_ref.shape)
    vt_ref[0] = lax.dot_general(wuv_ref[...], ckv, nt, preferred_element_type=F32).astype(BF16)
    cos_t = cos_t_ref[...]
    sin_t = sin_t_ref[...]
    q_nope_all = lax.dot_general(wqn_ref[...], cq, nt, preferred_element_type=F32) * scale
    q_rope_all = lax.dot_general(wqr_ref[...], cq, nt, preferred_element_type=F32)
    q_rot_all = lax.dot_general(wqb_ref[...], cq, nt, preferred_element_type=F32)
    last_row = lax.broadcasted_iota(jnp.int32, cos_t.shape, 0) == QK_ROPE_DIM - 1
    for h in range(N_HEADS):
        lo = h * QK_PAD_DIM
        rope_rows = slice(h * QK_ROPE_DIM, (h + 1) * QK_ROPE_DIM)
        q_nope = q_nope_all[h * QK_NOPE_DIM:(h + 1) * QK_NOPE_DIM]
        q_rope = (q_rope_all[rope_rows] * cos_t + q_rot_all[rope_rows] * sin_t) * scale
        q_norm = jnp.sqrt(jnp.sum(q_nope * q_nope, axis=0, keepdims=True)
                          + jnp.sum(q_rope * q_rope, axis=0, keepdims=True))
        qt_ref[0, lo:lo + QK_NOPE_DIM, :] = q_nope.astype(BF16)
        qt_ref[0, lo + QK_NOPE_DIM:lo + QK_NOPE_DIM + QK_ROPE_DIM, :] = q_rope.astype(BF16)
        qt_ref[0, lo + QK_NOPE_DIM + QK_ROPE_DIM:lo + QK_PAD_DIM, :] = jnp.where(
            last_row, q_norm, 0.0).astype(BF16)


def _attn_body(seq_len, qt_ref, k_ref, vt_ref, kmax_ref, o_ref):
    tq = ATTN_Q_BLOCK
    n_kv = seq_len // ATTN_KV_BLOCK
    stages = [(qb, j) for qb in range(ATTN_Q_BLOCKS_PER_STEP) for j in range(n_kv)]

    def scores(stage):
        qb, j = stage
        return _dot(k_ref[0, j * ATTN_KV_BLOCK:(j + 1) * ATTN_KV_BLOCK, :],
                    qt_ref[0, :, qb * tq:(qb + 1) * tq])

    def attend(shift):
        m = l = acc = None
        s_next = scores(stages[0])
        for idx, (qb, j) in enumerate(stages):
            s = s_next
            if idx + 1 < len(stages):
                s_next = scores(stages[idx + 1])
            vt = vt_ref[0, :, j * ATTN_KV_BLOCK:(j + 1) * ATTN_KV_BLOCK]
            if shift is not None:
                p = jnp.exp2(s - shift[:, qb * tq:(qb + 1) * tq])
                p_sum = jnp.sum(p, axis=0, keepdims=True)
                pv = _dot(vt, p.astype(BF16))
                l, acc = (p_sum, pv) if j == 0 else (l + p_sum, acc + pv)
            elif j == 0:
                m = jnp.max(s, axis=0, keepdims=True)
                p = jnp.exp2(s - m)
                l = jnp.sum(p, axis=0, keepdims=True)
                acc = _dot(vt, p.astype(BF16))
            else:
                m_new = jnp.maximum(m, jnp.max(s, axis=0, keepdims=True))
                alpha = jnp.exp2(m - m_new)
                p = jnp.exp2(s - m_new)
                l = alpha * l + jnp.sum(p, axis=0, keepdims=True)
                acc = alpha * acc + _dot(vt, p.astype(BF16))
                m = m_new
            if j == n_kv - 1:
                o_ref[0, qb * tq:(qb + 1) * tq, :] = (acc / l).T.astype(BF16)

    q_norm = qt_ref[0, QK_PAD_DIM - 1:QK_PAD_DIM, :].astype(F32)
    k_norm_max = NORM_BOUND_MARGIN * jnp.max(kmax_ref[...])
    bound = k_norm_max * q_norm
    bound_is_safe = k_norm_max * jnp.max(q_norm) <= MAX_SAFE_SHIFT

    @pl.when(bound_is_safe)
    def _():
        attend(bound)

    @pl.when(jnp.logical_not(bound_is_safe))
    def _():
        attend(None)


def _mla_attention(x, p, tabs):
    batch, seq_len, _ = x.shape
    tm = MLA_TOKENS
    qk_width = N_HEADS * QK_PAD_DIM
    v_width = N_HEADS * V_HEAD_DIM
    tile = lambda w: pl.BlockSpec((1, tm, w), lambda b, i: (b, i, 0))
    tile_t = lambda w: pl.BlockSpec((1, w, tm), lambda b, i: (b, 0, i))
    tab = pl.BlockSpec((tm, LANES), lambda b, i: (i, 0))
    tab_t = pl.BlockSpec((QK_ROPE_DIM, tm), lambda b, i: (0, i))
    qt, k, vt, kmax = pl.pallas_call(
        _mla_proj_body,
        grid=(batch, seq_len // tm),
        in_specs=[
            tile(D_MODEL),
            _const_spec((1, D_MODEL)),
            _const_spec((D_MODEL, Q_LORA_RANK)),
            _const_spec((1, Q_LORA_RANK)),
            _const_spec((N_HEADS * QK_NOPE_DIM, Q_LORA_RANK)),
            _const_spec((N_HEADS * QK_ROPE_DIM, Q_LORA_RANK)),
            _const_spec((N_HEADS * QK_ROPE_DIM, Q_LORA_RANK)),
            _const_spec((D_MODEL, KV_LORA_RANK + 2 * LANES)),
            _const_spec((1, KV_LORA_RANK)),
            _const_spec((KV_LORA_RANK, N_HEADS * QK_NOPE_DIM)),
            _const_spec((v_width, KV_LORA_RANK)),
            tab, tab, tab_t, tab_t,
        ],
        out_specs=[tile_t(qk_width), tile(qk_width), tile_t(v_width),
                   pl.BlockSpec((1, 1, SUBLANES, LANES), lambda b, i: (b, i, 0, 0))],
        out_shape=[
            jax.ShapeDtypeStruct((batch, qk_width, seq_len), BF16),
            jax.ShapeDtypeStruct((batch, seq_len, qk_width), BF16),
            jax.ShapeDtypeStruct((batch, v_width, seq_len), BF16),
            jax.ShapeDtypeStruct((batch, seq_len // tm, SUBLANES, LANES), F32),
        ],
        compiler_params=pltpu.CompilerParams(
            dimension_semantics=("parallel", "parallel"), vmem_limit_bytes=VMEM_LIMIT_BYTES),
        name="mla_proj",
    )(x, p["pre"], p["wdq"], p["qg"], p["wqn"], p["wqr"], p["wqb"], p["wkv"], p["kvg"], p["wuk"],
      p["wuv"], *tabs)

    tq = ATTN_Q_BLOCK * ATTN_Q_BLOCKS_PER_STEP
    attn = pl.pallas_call(
        functools.partial(_attn_body, seq_len),
        grid=(batch, N_HEADS, seq_len // tq),
        in_specs=[
            pl.BlockSpec((1, QK_PAD_DIM, tq), lambda b, h, i: (b, h, i)),
            pl.BlockSpec((1, seq_len, QK_PAD_DIM), lambda b, h, i: (b, 0, h)),
            pl.BlockSpec((1, V_HEAD_DIM, seq_len), lambda b, h, i: (b, h, 0)),
            pl.BlockSpec((1, seq_len // tm, SUBLANES, LANES), lambda b, h, i: (b, 0, 0, 0)),
        ],
        out_specs=pl.BlockSpec((1, tq, V_HEAD_DIM), lambda b, h, i: (b, i, h)),
        out_shape=jax.ShapeDtypeStruct((batch, seq_len, v_width), BF16),
        compiler_params=pltpu.CompilerParams(
            dimension_semantics=("parallel", "parallel", "parallel"),
            vmem_limit_bytes=VMEM_LIMIT_BYTES),
        name="mla_attention",
    )(qt, k, vt, kmax)

    return attn.reshape(batch * seq_len, v_width)


def _prep_mla(pre_g, w_dq, q_norm_g, w_uq, w_dkv, kv_norm_g, w_uk, w_uv, w_o, post_g):
    row = lambda v: v.reshape(1, -1)
    head_dim = QK_NOPE_DIM + QK_ROPE_DIM
    wq = w_uq.reshape(Q_LORA_RANK, N_HEADS, head_dim)
    q_x1 = wq[:, :, QK_NOPE_DIM:QK_NOPE_DIM + ROPE_HALF]
    q_x2 = wq[:, :, QK_NOPE_DIM + ROPE_HALF:]
    wqn = wq[:, :, :QK_NOPE_DIM]
    wqr = wq[:, :, QK_NOPE_DIM:]
    wqb = jnp.concatenate([-q_x2, q_x1], axis=-1)
    k_x1 = w_dkv[:, KV_LORA_RANK:KV_LORA_RANK + ROPE_HALF]
    k_x2 = w_dkv[:, KV_LORA_RANK + ROPE_HALF:]
    kpad = jnp.zeros((D_MODEL, LANES - QK_ROPE_DIM), F32)
    wkv = jnp.concatenate([w_dkv[:, :KV_LORA_RANK], k_x1, k_x2, kpad, -k_x2, k_x1, kpad], axis=-1)
    return dict(
        pre=row(pre_g), wdq=w_dq.astype(BF16), qg=row(q_norm_g),
        wqn=wqn.reshape(Q_LORA_RANK, N_HEADS * QK_NOPE_DIM).T.astype(BF16),
        wqr=wqr.reshape(Q_LORA_RANK, N_HEADS * QK_ROPE_DIM).T.astype(BF16),
        wqb=wqb.reshape(Q_LORA_RANK, N_HEADS * QK_ROPE_DIM).T.astype(BF16),
        wkv=wkv.astype(BF16), kvg=row(kv_norm_g), wuk=w_uk.astype(BF16),
        wuv=w_uv.T.astype(BF16), wo=w_o.astype(BF16), post=row(post_g))


def _rope_tables(seq_len):
    pos = jnp.arange(seq_len, dtype=F32)
    inv_freq = ROPE_THETA ** (-jnp.arange(0, QK_ROPE_DIM, 2, dtype=F32) / QK_ROPE_DIM)
    ang = pos[:, None] * inv_freq[None, :]
    zeros = jnp.zeros((seq_len, LANES - QK_ROPE_DIM), F32)
    cos, sin = jnp.cos(ang), jnp.sin(ang)
    cos_tab = jnp.concatenate([cos, cos, zeros], axis=-1)
    sin_tab = jnp.concatenate([sin, sin, zeros], axis=-1)
    return cos_tab, sin_tab, cos_tab[:, :QK_ROPE_DIM].T, sin_tab[:, :QK_ROPE_DIM].T


def kernel(x_prompt, x_sample, l0_ffn1_pre_g, l0_ffn1_w_gate, l0_ffn1_w_up, l0_ffn1_w_down, l0_ffn1_post_g, l0_mix_pre_g, l0_conv_w_pw1, l0_conv_b_pw1, l0_conv_w_dw, l0_conv_b_dw, l0_conv_ln_g, l0_conv_ln_b, l0_conv_w_pw2, l0_conv_b_pw2, l0_mix_post_g, l0_ffn2_pre_g, l0_ffn2_w_gate, l0_ffn2_w_up, l0_ffn2_w_down, l0_ffn2_post_g, l1_ffn1_pre_g, l1_ffn1_w_gate, l1_ffn1_w_up, l1_ffn1_w_down, l1_ffn1_post_g, l1_mix_pre_g, l1_mla_w_dq, l1_mla_q_norm_g, l1_mla_w_uq, l1_mla_w_dkv, l1_mla_kv_norm_g, l1_mla_w_uk, l1_mla_w_uv, l1_mla_w_o, l1_mix_post_g, l1_ffn2_pre_g, l1_ffn2_w_gate, l1_ffn2_w_up, l1_ffn2_w_down, l1_ffn2_post_g):
    l0_ffn1 = _prep_ffn(l0_ffn1_pre_g, l0_ffn1_w_gate, l0_ffn1_w_up, l0_ffn1_w_down, l0_ffn1_post_g)
    l0_ffn2 = _prep_ffn(l0_ffn2_pre_g, l0_ffn2_w_gate, l0_ffn2_w_up, l0_ffn2_w_down, l0_ffn2_post_g)
    l1_ffn1 = _prep_ffn(l1_ffn1_pre_g, l1_ffn1_w_gate, l1_ffn1_w_up, l1_ffn1_w_down, l1_ffn1_post_g)
    l1_ffn2 = _prep_ffn(l1_ffn2_pre_g, l1_ffn2_w_gate, l1_ffn2_w_up, l1_ffn2_w_down, l1_ffn2_post_g)
    conv = _prep_conv(l0_mix_pre_g, l0_conv_w_pw1, l0_conv_b_pw1, l0_conv_w_dw, l0_conv_b_dw,
                      l0_conv_ln_g, l0_conv_ln_b, l0_conv_w_pw2, l0_conv_b_pw2, l0_mix_post_g)
    mla = _prep_mla(l1_mix_pre_g, l1_mla_w_dq, l1_mla_q_norm_g, l1_mla_w_uq, l1_mla_w_dkv,
                    l1_mla_kv_norm_g, l1_mla_w_uk, l1_mla_w_uv, l1_mla_w_o, l1_mix_post_g)

    def trunk(x):
        shape = x.shape
        flat = lambda a: a.reshape(-1, D_MODEL)
        tabs = _rope_tables(shape[1])
        x = _ffn(flat(x), l0_ffn1).reshape(shape)
        x = _conv(x, conv)
        x = _ffn(flat(x), l0_ffn2)
        x = _ffn(x, l1_ffn1).reshape(shape)
        attn = _mla_attention(x, mla, tabs)
        x = _ffn(flat(x), l1_ffn2, attn=(attn, mla["wo"], mla["post"])).reshape(shape)
        return x

    return (trunk(x_prompt), trunk(x_sample))
```
